```python
import math
import jax, jax.numpy as jnp
from jax import lax
import numpy as np

D_MODEL = 2048
BATCH = 1
SEQ = 8192
DEPTH = 1
DEC_BATCH = 32
DEC_SEQ = 4
PAST_LEN = 8192
PAGE_SIZE = 128

MIX_WIDTH = D_MODEL
ATTN_WIDTH = MIX_WIDTH // 2
SSM_WIDTH = MIX_WIDTH - ATTN_WIDTH
N_HEADS = 8
HEAD_DIM = ATTN_WIDTH // N_HEADS
HALF_DIM = HEAD_DIM // 2
N_BUCKETS = 32
MAX_DISTANCE = 128
SSM_GROUP = 16
N_SSM_GROUPS = SSM_WIDTH // SSM_GROUP
SSM_STATE = 64
IN_WIDTH = 3 * ATTN_WIDTH + SSM_WIDTH
PEER_HEADS = 8
PEER_NKEYS = 128
PEER_EXPERTS = PEER_NKEYS * PEER_NKEYS
PEER_TOPK = 16
PEER_KEY_DIM = 256
PEER_HALF = PEER_KEY_DIM // 2
Q_BLOCK = 128
TOK_BLOCK = 128
LN_EPS = 1e-5
DEEPNORM_ALPHA = (2.0 * DEPTH) ** 0.25
DEEPNORM_BETA = (8.0 * DEPTH) ** -0.25

kernel_name = "hymba_diffattn_s5_peer_deepnorm_step"


def layer_norm(x, g, b):
    xf = x.astype(jnp.float32)
    mu = jnp.mean(xf, axis=-1, keepdims=True)
    var = jnp.mean(jnp.square(xf - mu), axis=-1, keepdims=True)
    y = (xf - mu) * lax.rsqrt(var + LN_EPS) * g.astype(jnp.float32) + b.astype(jnp.float32)
    return y.astype(x.dtype)


def t5_bucket(q_pos, k_pos):
    n = jnp.maximum(q_pos[:, None] - k_pos[None, :], 0)
    max_exact = N_BUCKETS // 2
    nf = jnp.maximum(n, 1).astype(jnp.float32)
    large = max_exact + (jnp.log(nf / max_exact) / math.log(MAX_DISTANCE / max_exact)
                         * (N_BUCKETS - max_exact)).astype(jnp.int32)
    large = jnp.minimum(large, N_BUCKETS - 1)
    return jnp.where(n < max_exact, n, large)


def diff_attention(q, k, v, q_pos, k_pos, rel_bias, lam):
    f32 = jnp.float32
    qf = q.astype(f32).reshape(q.shape[0], q.shape[1], N_HEADS, 2, HALF_DIM)
    kf = k.astype(f32).reshape(k.shape[0], k.shape[1], N_HEADS, 2, HALF_DIM)
    logits = jnp.einsum('bqhmd,bkhmd->bmhqk', qf, kf) * (HALF_DIM ** -0.5)
    bias = rel_bias.astype(f32)[t5_bucket(q_pos, k_pos)]
    logits = logits + jnp.transpose(bias, (2, 0, 1))[None, None]
    mask = k_pos[None, :] <= q_pos[:, None]
    logits = jnp.where(mask, logits, -jnp.inf)
    p = jax.nn.softmax(logits, axis=-1)
    attn = p[:, 0] - lam * p[:, 1]
    return jnp.einsum('bhqk,bkhd->bqhd', attn, v.astype(f32))


def attend_prompt(q, k, v, rel_bias, lam):
    b, s = q.shape[0], q.shape[1]
    nb = s // Q_BLOCK
    pos = jnp.arange(s, dtype=jnp.int32)
    qb = q.reshape(b, nb, Q_BLOCK, N_HEADS, HEAD_DIM).transpose(1, 0, 2, 3, 4)
    qpb = pos.reshape(nb, Q_BLOCK)
    out = lax.map(lambda a: diff_attention(a[0], k, v, a[1], pos, rel_bias, lam), (qb, qpb))
    return out.transpose(1, 0, 2, 3, 4).reshape(b, s, N_HEADS, HEAD_DIM)


def s5_scan(u, h0_re, h0_im, a_re, a_im, b_re, b_im, c_re, c_im, d, log_dt):
    f32 = jnp.float32
    bsz, L = u.shape[0], u.shape[1]
    ug = u.astype(f32).reshape(bsz, L, N_SSM_GROUPS, SSM_GROUP)
    dt = jnp.exp(log_dt.astype(f32))[:, None]
    ar, ai = a_re.astype(f32), a_im.astype(f32)
    mag = jnp.exp(ar * dt)
    abar_re, abar_im = mag * jnp.cos(ai * dt), mag * jnp.sin(ai * dt)
    nr, ni = abar_re - 1.0, abar_im
    den = ar * ar + ai * ai
    f_re = (nr * ar + ni * ai) / den
    f_im = (ni * ar - nr * ai) / den
    br, bi = b_re.astype(f32), b_im.astype(f32)
    bb_re = f_re[..., None] * br - f_im[..., None] * bi
    bb_im = f_re[..., None] * bi + f_im[..., None] * br
    bu_re = jnp.einsum('blgc,gpc->blgp', ug, bb_re)
    bu_im = jnp.einsum('blgc,gpc->blgp', ug, bb_im)
    h0r, h0i = h0_re.astype(f32), h0_im.astype(f32)
    bu_re = bu_re.at[:, 0].add(abar_re * h0r - abar_im * h0i)
    bu_im = bu_im.at[:, 0].add(abar_re * h0i + abar_im * h0r)
    a_b_re = jnp.broadcast_to(abar_re, bu_re.shape)
    a_b_im = jnp.broadcast_to(abar_im, bu_im.shape)

    def combine(e1, e2):
        a1r, a1i, b1r, b1i = e1
        a2r, a2i, b2r, b2i = e2
        return (a2r * a1r - a2i * a1i, a2r * a1i + a2i * a1r,
                a2r * b1r - a2i * b1i + b2r, a2r * b1i + a2i * b1r + b2i)

    _, _, h_re, h_im = lax.associative_scan(combine, (a_b_re, a_b_im, bu_re, bu_im), axis=1)
    y = (jnp.einsum('blgp,gcp->blgc', h_re, c_re.astype(f32))
         - jnp.einsum('blgp,gcp->blgc', h_im, c_im.astype(f32))
         + d.astype(f32) * ug)
    return y.reshape(bsz, L, SSM_WIDTH), h_re[:, -1], h_im[:, -1]


def in_proj(x, w_in):
    proj = jnp.einsum('bld,de->ble', x, w_in)
    b, L = x.shape[0], x.shape[1]
    q = proj[..., :ATTN_WIDTH].reshape(b, L, N_HEADS, HEAD_DIM)
    k = proj[..., ATTN_WIDTH:2 * ATTN_WIDTH].reshape(b, L, N_HEADS, HEAD_DIM)
    v = proj[..., 2 * ATTN_WIDTH:3 * ATTN_WIDTH].reshape(b, L, N_HEADS, HEAD_DIM)
    u = proj[..., 3 * ATTN_WIDTH:]
    return q, k, v, u


def merge_groups(attn, y_ssm, subln_g, lam_init, glu_w, glu_b, w_out, dtype):
    b, L = attn.shape[0], attn.shape[1]
    o = attn * lax.rsqrt(jnp.mean(jnp.square(attn), axis=-1, keepdims=True) + LN_EPS)
    o = o * subln_g.astype(jnp.float32) * (1.0 - lam_init)
    z = jax.nn.gelu(y_ssm)
    z = z * jax.nn.sigmoid(z @ glu_w.astype(jnp.float32) + glu_b.astype(jnp.float32))
    mix = jnp.concatenate([o.reshape(b, L, ATTN_WIDTH), z], axis=-1).astype(dtype)
    return jnp.einsum('ble,ed->bld', mix, w_out)


def peer_ffn(x, w_query, sub_keys, u_tab, v_tab):
    f32 = jnp.float32
    bsz, L, D = x.shape
    T = bsz * L
    pad = (-T) % TOK_BLOCK
    xf = jnp.pad(x.reshape(T, D), ((0, pad), (0, 0)))
    nb = (T + pad) // TOK_BLOCK

    def block(xb):
        q = (xb @ w_query).astype(f32).reshape(TOK_BLOCK, PEER_HEADS, 2, PEER_HALF)
        s = jnp.einsum('thmd,hmnd->thmn', q, sub_keys.astype(f32))
        s_top, i_top = lax.top_k(s, PEER_TOPK)
        cand = s_top[:, :, 0, :, None] + s_top[:, :, 1, None, :]
        cand_idx = i_top[:, :, 0, :, None] * PEER_NKEYS + i_top[:, :, 1, None, :]
        best, sel = lax.top_k(cand.reshape(TOK_BLOCK, PEER_HEADS, PEER_TOPK * PEER_TOPK), PEER_TOPK)
        expert = jnp.take_along_axis(
            cand_idx.reshape(TOK_BLOCK, PEER_HEADS, PEER_TOPK * PEER_TOPK), sel, axis=-1)
        g = jax.nn.softmax(best, axis=-1)
        ue = u_tab[expert]
        ve = v_tab[expert]
        h = jax.nn.gelu(jnp.einsum('td,thkd->thk', xb, ue).astype(f32))
        return jnp.einsum('thk,thkd->td', (g * h).astype(xb.dtype), ve)

    out = lax.map(block, xf.reshape(nb, TOK_BLOCK, D))
    return out.reshape(nb * TOK_BLOCK, D)[:T].reshape(bsz, L, D)


def setup_inputs(seed: int = 0) -> dict:
    key = jax.random.key(seed)
    ks = jax.random.split(key, 40)
    f32 = jnp.float32
    n_pages = PAST_LEN // PAGE_SIZE
    used = DEC_BATCH * n_pages
    n_pool = used + max(1, used // 4)
    nrm = lambda k, shape, s: jax.random.normal(k, shape, f32) * s
    page_table = jax.random.permutation(ks[0], n_pool)[:used].reshape(DEC_BATCH, n_pages).astype(jnp.int32)
    a_re = -0.5 + 0.01 * jax.random.normal(ks[1], (DEPTH, N_SSM_GROUPS, SSM_STATE), f32)
    a_im = (math.pi * jnp.arange(SSM_STATE, dtype=f32))[None, None, :] + 0.01 * jax.random.normal(
        ks[2], (DEPTH, N_SSM_GROUPS, SSM_STATE), f32)
    log_dt = jax.random.uniform(ks[3], (DEPTH, N_SSM_GROUPS), f32, math.log(1e-3), math.log(1e-1))
    return {
        "x_prompt": nrm(ks[4], (BATCH, SEQ, D_MODEL), 1.0),
        "x_sample": nrm(ks[5], (DEC_BATCH, DEC_SEQ, D_MODEL), 1.0),
        "cache_k": nrm(ks[6], (DEPTH, n_pool, PAGE_SIZE, N_HEADS, HEAD_DIM), 1.0),
        "cache_v": nrm(ks[7], (DEPTH, n_pool, PAGE_SIZE, N_HEADS, HEAD_DIM), 1.0),
        "state_ssm_re": nrm(ks[8], (DEPTH, DEC_BATCH, N_SSM_GROUPS, SSM_STATE), 0.5),
        "state_ssm_im": nrm(ks[9], (DEPTH, DEC_BATCH, N_SSM_GROUPS, SSM_STATE), 0.5),
        "page_table": page_table,
        "w_in": nrm(ks[10], (DEPTH, D_MODEL, IN_WIDTH), D_MODEL ** -0.5),
        "lambda_q1": nrm(ks[11], (DEPTH, HALF_DIM), 0.1),
        "lambda_k1": nrm(ks[12], (DEPTH, HALF_DIM), 0.1),
        "lambda_q2": nrm(ks[13], (DEPTH, HALF_DIM), 0.1),
        "lambda_k2": nrm(ks[14], (DEPTH, HALF_DIM), 0.1),
        "attn_subln_g": 1.0 + nrm(ks[15], (DEPTH, HEAD_DIM), 0.02),
        "rel_bias": nrm(ks[16], (N_BUCKETS, N_HEADS), 0.5),
        "ssm_a_re": a_re,
        "ssm_a_im": a_im,
        "ssm_b_re": nrm(ks[17], (DEPTH, N_SSM_GROUPS, SSM_STATE, SSM_GROUP), (2 * SSM_GROUP) ** -0.5),
        "ssm_b_im": nrm(ks[18], (DEPTH, N_SSM_GROUPS, SSM_STATE, SSM_GROUP), (2 * SSM_GROUP) ** -0.5),
        "ssm_c_re": nrm(ks[19], (DEPTH, N_SSM_GROUPS, SSM_GROUP, SSM_STATE), (2 * SSM_STATE) ** -0.5),
        "ssm_c_im": nrm(ks[20], (DEPTH, N_SSM_GROUPS, SSM_GROUP, SSM_STATE), (2 * SSM_STATE) ** -0.5),
        "ssm_d": nrm(ks[21], (DEPTH, N_SSM_GROUPS, SSM_GROUP), 1.0),
        "ssm_log_dt": log_dt,
        "ssm_glu_w": nrm(ks[22], (DEPTH, SSM_WIDTH, SSM_WIDTH), SSM_WIDTH ** -0.5),
        "ssm_glu_b": nrm(ks[23], (DEPTH, SSM_WIDTH), 0.02),
        "w_out": nrm(ks[24], (DEPTH, MIX_WIDTH, D_MODEL), DEEPNORM_BETA * MIX_WIDTH ** -0.5),
        "ln1_g": 1.0 + nrm(ks[25], (DEPTH, D_MODEL), 0.02),
        "ln1_b": nrm(ks[26], (DEPTH, D_MODEL), 0.02),
        "peer_w_query": nrm(ks[27], (DEPTH, D_MODEL, PEER_HEADS * PEER_KEY_DIM), D_MODEL ** -0.5),
        "peer_sub_keys": nrm(ks[28], (DEPTH, PEER_HEADS, 2, PEER_NKEYS, PEER_HALF), PEER_HALF ** -0.5),
        "peer_u": nrm(ks[29], (DEPTH, PEER_EXPERTS, D_MODEL), D_MODEL ** -0.5),
        "peer_v": nrm(ks[30], (DEPTH, PEER_EXPERTS, D_MODEL), DEEPNORM_BETA * PEER_HEADS ** -0.5),
        "ln2_g": 1.0 + nrm(ks[31], (DEPTH, D_MODEL), 0.02),
        "ln2_b": nrm(ks[32], (DEPTH, D_MODEL), 0.02),
    }


def reference(x_prompt, x_sample, cache_k, cache_v, state_ssm_re, state_ssm_im, page_table,
              w_in, lambda_q1, lambda_k1, lambda_q2, lambda_k2, attn_subln_g, rel_bias,
              ssm_a_re, ssm_a_im, ssm_b_re, ssm_b_im, ssm_c_re, ssm_c_im, ssm_d, ssm_log_dt,
              ssm_glu_w, ssm_glu_b, w_out, ln1_g, ln1_b,
              peer_w_query, peer_sub_keys, peer_u, peer_v, ln2_g, ln2_b):
    f32 = jnp.float32
    dec_b, dec_s = x_sample.shape[0], x_sample.shape[1]
    n_pages = page_table.shape[1]
    past_len = n_pages * PAGE_SIZE
    q_pos_s = past_len + jnp.arange(dec_s, dtype=jnp.int32)
    k_pos_s = jnp.arange(past_len + dec_s, dtype=jnp.int32)
    xp, xs = x_prompt, x_sample
    kp_l, vp_l, hrp_l, hip_l, ks_l, vs_l, hrs_l, his_l = [], [], [], [], [], [], [], []
    for l in range(DEPTH):
        lam_init = 0.8 - 0.6 * math.exp(-0.3 * l)
        lam = (jnp.exp(jnp.sum(lambda_q1[l].astype(f32) * lambda_k1[l].astype(f32)))
               - jnp.exp(jnp.sum(lambda_q2[l].astype(f32) * lambda_k2[l].astype(f32))) + lam_init)
        ssm_p = (ssm_a_re[l], ssm_a_im[l], ssm_b_re[l], ssm_b_im[l], ssm_c_re[l], ssm_c_im[l],
                 ssm_d[l], ssm_log_dt[l])

        q, k, v, u = in_proj(xp, w_in[l])
        attn = attend_prompt(q, k, v, rel_bias, lam)
        h0 = jnp.zeros((xp.shape[0], N_SSM_GROUPS, SSM_STATE), f32)
        y_ssm, hr, hi = s5_scan(u, h0, h0, *ssm_p)
        mix = merge_groups(attn, y_ssm, attn_subln_g[l], lam_init, ssm_glu_w[l], ssm_glu_b[l], w_out[l], xp.dtype)
        x1 = layer_norm(DEEPNORM_ALPHA * xp + mix, ln1_g[l], ln1_b[l])
        xp = layer_norm(DEEPNORM_ALPHA * x1 + peer_ffn(x1, peer_w_query[l], peer_sub_keys[l], peer_u[l], peer_v[l]),
                        ln2_g[l], ln2_b[l])
        kp_l.append(k); vp_l.append(v); hrp_l.append(hr); hip_l.append(hi)

        q, k, v, u = in_proj(xs, w_in[l])
        k_past = cache_k[l][page_table].reshape(dec_b, past_len, N_HEADS, HEAD_DIM).astype(k.dtype)
        v_past = cache_v[l][page_table].reshape(dec_b, past_len, N_HEADS, HEAD_DIM).astype(v.dtype)
        k_all = jnp.concatenate([k_past, k], axis=1)
        v_all = jnp.concatenate([v_past, v], axis=1)
        attn = diff_attention(q, k_all, v_all, q_pos_s, k_pos_s, rel_bias, lam)
        y_ssm, hr, hi = s5_scan(u, state_ssm_re[l], state_ssm_im[l], *ssm_p)
        mix = merge_groups(attn, y_ssm, attn_subln_g[l], lam_init, ssm_glu_w[l], ssm_glu_b[l], w_out[l], xs.dtype)
        x1 = layer_norm(DEEPNORM_ALPHA * xs + mix, ln1_g[l], ln1_b[l])
        xs = layer_norm(DEEPNORM_ALPHA * x1 + peer_ffn(x1, peer_w_query[l], peer_sub_keys[l], peer_u[l], peer_v[l]),
                        ln2_g[l], ln2_b[l])
        ks_l.append(k); vs_l.append(v); hrs_l.append(hr); his_l.append(hi)

    k_prompt, v_prompt = jnp.stack(kp_l), jnp.stack(vp_l)
    ssm_re_prompt, ssm_im_prompt = jnp.stack(hrp_l), jnp.stack(hip_l)
    k_sample, v_sample = jnp.stack(ks_l), jnp.stack(vs_l)
    ssm_re_sample, ssm_im_sample = jnp.stack(hrs_l), jnp.stack(his_l)
    return (xp, xs, k_prompt, v_prompt, ssm_re_prompt, ssm_im_prompt,
            k_sample, v_sample, ssm_re_sample, ssm_im_sample)
```

```python
import functools
import math

import numpy as np
import jax
import jax.numpy as jnp
from jax import lax
from jax.experimental import pallas as pl
from jax.experimental.pallas import tpu as pltpu

F32 = jnp.float32
BF16 = jnp.bfloat16

D_MODEL = 2048
PAGE_SIZE = 128
ATTN_WIDTH = 1024
SSM_WIDTH = 1024
N_HEADS = 8
HEAD_DIM = 128
HALF_DIM = 64
N_BUCKETS = 32
MAX_DISTANCE = 128
SSM_GROUP = 16
N_SSM_GROUPS = 64
SSM_STATE = 64
PEER_HEADS = 8
PEER_NKEYS = 128
PEER_TOPK = 16
LN_EPS = 1e-5
DEPTH = 1
DEEPNORM_ALPHA = (2.0 * DEPTH) ** 0.25

LANES = 128
SCAN_CHUNK = 16
NEG = -1e30
VMEM_LIMIT = 56 * 1024 * 1024

_HI = lax.Precision.HIGHEST


def _cparams(sem):
    return pltpu.CompilerParams(dimension_semantics=sem, vmem_limit_bytes=VMEM_LIMIT)


def _rep(x, n):
    return x if n == 1 else jnp.concatenate([x] * n, axis=1)


def _proj_kernel(x_ref, w_ref, o_ref):
    o_ref[0] = jnp.dot(x_ref[...].astype(BF16), w_ref[...], preferred_element_type=F32)


def _proj(x, w, tm, tn):
    t, k = x.shape
    n = w.shape[1]
    return pl.pallas_call(
        _proj_kernel,
        grid=(t // tm, n // tn),
        in_specs=[pl.BlockSpec((tm, k), lambda i, j: (i, 0)),
                  pl.BlockSpec((k, tn), lambda i, j: (0, j))],
        out_specs=pl.BlockSpec((1, tm, tn), lambda i, j: (j, i, 0)),
        out_shape=jax.ShapeDtypeStruct((n // tn, t, tn), F32),
        compiler_params=_cparams(("parallel", "arbitrary")),
        name="in_proj",
    )(x, w)


def _bucket_np(d):
    n = np.maximum(d, 0)
    max_exact = N_BUCKETS // 2
    nf = np.maximum(n, 1).astype(np.float64)
    large = max_exact + (np.log(nf / max_exact) / math.log(MAX_DISTANCE / max_exact)
                         * (N_BUCKETS - max_exact)).astype(np.int32)
    large = np.minimum(large, N_BUCKETS - 1)
    return np.where(n < max_exact, n, large).astype(np.int32)


def _bias_kernel(rb_ref, bkt_ref, o_ref):
    h = pl.program_id(0)
    bkt = bkt_ref[0]
    far = rb_ref[N_BUCKETS - 1, h]
    acc = jnp.where(bkt < 0, NEG, 0.0).astype(F32)
    for b in range(N_BUCKETS - 1):
        acc = jnp.where(bkt == b, rb_ref[b, h] - far, acc)
    o_ref[0, 0] = acc


def _bias_tiles(rel_bias, bkt):
    n, r, c = bkt.shape
    return pl.pallas_call(
        _bias_kernel,
        grid=(N_HEADS, n),
        in_specs=[pl.BlockSpec(memory_space=pltpu.SMEM),
                  pl.BlockSpec((1, r, c), lambda h, i: (i, 0, 0))],
        out_specs=pl.BlockSpec((1, 1, r, c), lambda h, i: (h, i, 0, 0)),
        out_shape=jax.ShapeDtypeStruct((N_HEADS, n, r, c), F32),
        compiler_params=_cparams(("parallel", "arbitrary")),
        name="bias_tiles",
    )(rel_bias, jnp.asarray(bkt))


def _subln(attn, g_row, lam_init):
    ms = jnp.mean(jnp.square(attn), axis=-1, keepdims=True)
    return attn * lax.rsqrt(ms + LN_EPS) * g_row * (1.0 - lam_init)


def _attn_kernel(qi_ref, ki_ref, lam_ref, q_ref, k_ref, v_ref, bias_ref, g_ref, o_ref,
                 q2_sc, m_sc, l_sc, acc_sc, *, tb, lam_init):
    s = pl.program_id(1)
    qi = qi_ref[s]
    ki = ki_ref[s]

    @pl.when(ki == 0)
    def _init():
        q = q_ref[0] * (HALF_DIM ** -0.5)
        lane = lax.broadcasted_iota(jnp.int32, q.shape, 1)
        q2_sc[0:tb, :] = jnp.where(lane < HALF_DIM, q, 0.0).astype(BF16)
        q2_sc[tb:2 * tb, :] = jnp.where(lane >= HALF_DIM, q, 0.0).astype(BF16)
        m_sc[...] = jnp.full(m_sc.shape, NEG, F32)
        l_sc[...] = jnp.zeros(l_sc.shape, F32)
        acc_sc[...] = jnp.zeros(acc_sc.shape, F32)

    def logits():
        kb = k_ref[0].astype(BF16)
        return lax.dot_general(q2_sc[...], kb, (((1,), (1,)), ((), ())),
                               preferred_element_type=F32)

    def update(sc):
        m_prev = m_sc[...]
        m_new = jnp.maximum(m_prev, jnp.max(sc, axis=-1, keepdims=True))
        alpha = jnp.exp(m_prev - m_new)
        p = jnp.exp(sc - _rep(m_new, tb // LANES))
        l_sc[...] = alpha * l_sc[...] + jnp.sum(p, axis=-1, keepdims=True)
        acc_sc[...] = alpha * acc_sc[...] + jnp.dot(
            p.astype(BF16), v_ref[0].astype(BF16), preferred_element_type=F32)
        m_sc[...] = m_new

    @pl.when(ki < qi - 1)
    def _far():
        update(logits())

    @pl.when(ki == qi - 1)
    def _sub():
        b = bias_ref[0, 1]
        update(logits() + jnp.concatenate([b, b], axis=0))

    @pl.when(ki == qi)
    def _diag():
        b = bias_ref[0, 0]
        update(logits() + jnp.concatenate([b, b], axis=0))
        o = acc_sc[...] / l_sc[...]
        attn = o[0:tb] - lam_ref[0] * o[tb:2 * tb]
        o_ref[...] = _subln(attn, g_ref[...], lam_init)


def _attn_prompt(qkvu, bias, lam, subln_g, tb, lam_init):
    t = qkvu.shape[1]
    nb = t // tb
    qi_idx = np.concatenate([np.full(i + 1, i) for i in range(nb)]).astype(np.int32)
    ki_idx = np.concatenate([np.arange(i + 1) for i in range(nb)]).astype(np.int32)
    grid_spec = pltpu.PrefetchScalarGridSpec(
        num_scalar_prefetch=2,
        grid=(N_HEADS, len(qi_idx)),
        in_specs=[
            pl.BlockSpec(memory_space=pltpu.SMEM),
            pl.BlockSpec((1, tb, HEAD_DIM), lambda h, s, qi, ki: (0, qi[s], h)),
            pl.BlockSpec((1, tb, HEAD_DIM), lambda h, s, qi, ki: (1, ki[s], h)),
            pl.BlockSpec((1, tb, HEAD_DIM), lambda h, s, qi, ki: (2, ki[s], h)),
            pl.BlockSpec((1, 2, tb, tb), lambda h, s, qi, ki: (h, 0, 0, 0)),
            pl.BlockSpec((1, HEAD_DIM), lambda h, s, qi, ki: (0, 0)),
        ],
        out_specs=pl.BlockSpec((tb, HEAD_DIM), lambda h, s, qi, ki: (qi[s], h)),
        scratch_shapes=[
            pltpu.VMEM((2 * tb, HEAD_DIM), BF16),
            pltpu.VMEM((2 * tb, LANES), F32),
            pltpu.VMEM((2 * tb, LANES), F32),
            pltpu.VMEM((2 * tb, HEAD_DIM), F32),
        ],
    )
    return pl.pallas_call(
        functools.partial(_attn_kernel, tb=tb, lam_init=lam_init),
        grid_spec=grid_spec,
        out_shape=jax.ShapeDtypeStruct((t, ATTN_WIDTH), F32),
        compiler_params=_cparams(("parallel", "arbitrary")),
        name="attn_prompt",
    )(jnp.asarray(qi_idx), jnp.asarray(ki_idx), lam, qkvu, qkvu, qkvu, bias, subln_g)


PPS = 8
QROWS = 8
NEWK = 16


def _decode_kernel(pt_ref, lam_ref, q2_ref, kn_ref, vn_ref, bl_ref, bn_ref, g_ref, *rest,
                   n_steps, lam_init):
    k_refs = rest[0:PPS]
    v_refs = rest[PPS:2 * PPS]
    o_ref = rest[2 * PPS]
    m_sc, l_sc, acc_sc = rest[2 * PPS + 1:]
    p_idx = pl.program_id(1)
    last = p_idx == n_steps - 1

    @pl.when(p_idx == 0)
    def _init():
        m_sc[...] = jnp.full(m_sc.shape, NEG, F32)
        l_sc[...] = jnp.zeros(l_sc.shape, F32)
        acc_sc[...] = jnp.zeros(acc_sc.shape, F32)

    last_f = jnp.where(last, 1.0, 0.0).astype(F32)

    def update(h, sc, vmat):
        m_prev = m_sc[h]
        m_new = jnp.maximum(m_prev, jnp.max(sc, axis=-1, keepdims=True))
        alpha = jnp.exp(m_prev - m_new)
        p = jnp.exp(sc - _rep(m_new, sc.shape[1] // LANES) if sc.shape[1] >= LANES
                    else sc - m_new[:, 0:sc.shape[1]])
        l_sc[h] = alpha * l_sc[h] + jnp.sum(p, axis=-1, keepdims=True)
        acc_sc[h] = alpha * acc_sc[h] + jnp.dot(p.astype(BF16), vmat,
                                                 preferred_element_type=F32)
        m_sc[h] = m_new

    for h in range(N_HEADS):
        q2 = q2_ref[0, h].astype(BF16)
        kcat = jnp.concatenate([r[0, 0, :, h, :] for r in k_refs], axis=0).astype(BF16)
        vcat = jnp.concatenate([r[0, 0, :, h, :] for r in v_refs], axis=0).astype(BF16)
        sc = lax.dot_general(q2, kcat, (((1,), (1,)), ((), ())), preferred_element_type=F32)
        sc = sc + bl_ref[h] * last_f
        update(h, sc, vcat)

    @pl.when(last)
    def _finish():
        for h in range(N_HEADS):
            q2 = q2_ref[0, h].astype(BF16)
            kn = kn_ref[0, h].astype(BF16)
            vn = vn_ref[0, h].astype(BF16)
            sc = lax.dot_general(q2, kn, (((1,), (1,)), ((), ())), preferred_element_type=F32)
            update(h, sc + bn_ref[h], vn)
            o = acc_sc[h] / l_sc[h]
            half = QROWS // 2
            attn = o[0:half] - lam_ref[0] * o[half:QROWS]
            o_ref[0, :, h * HEAD_DIM:(h + 1) * HEAD_DIM] = _subln(attn, g_ref[...], lam_init)


def _attn_sample(q2, k_new, v_new, cache_k, cache_v, page_table, bias_last, bias_new,
                 lam, subln_g, lam_init):
    nb, n_pages = page_table.shape
    n_steps = n_pages // PPS
    dec_seq = QROWS // 2

    def page_spec(j):
        return pl.BlockSpec(
            (1, 1, PAGE_SIZE, N_HEADS, HEAD_DIM),
            lambda b, p, pt, j=j: (0, pt[b * n_pages + p * PPS + j], 0, 0, 0))

    grid_spec = pltpu.PrefetchScalarGridSpec(
        num_scalar_prefetch=1,
        grid=(nb, n_steps),
        in_specs=[
            pl.BlockSpec(memory_space=pltpu.SMEM),
            pl.BlockSpec((1, N_HEADS, QROWS, HEAD_DIM), lambda b, p, pt: (b, 0, 0, 0)),
            pl.BlockSpec((1, N_HEADS, NEWK, HEAD_DIM), lambda b, p, pt: (b, 0, 0, 0)),
            pl.BlockSpec((1, N_HEADS, NEWK, HEAD_DIM), lambda b, p, pt: (b, 0, 0, 0)),
            pl.BlockSpec((N_HEADS, QROWS, PPS * PAGE_SIZE), lambda b, p, pt: (0, 0, 0)),
            pl.BlockSpec((N_HEADS, QROWS, NEWK), lambda b, p, pt: (0, 0, 0)),
            pl.BlockSpec((1, HEAD_DIM), lambda b, p, pt: (0, 0)),
        ] + [page_spec(j) for j in range(PPS)] + [page_spec(j) for j in range(PPS)],
        out_specs=pl.BlockSpec((1, dec_seq, ATTN_WIDTH), lambda b, p, pt: (b, 0, 0)),
        scratch_shapes=[
            pltpu.VMEM((N_HEADS, QROWS, LANES), F32),
            pltpu.VMEM((N_HEADS, QROWS, LANES), F32),
            pltpu.VMEM((N_HEADS, QROWS, HEAD_DIM), F32),
        ],
    )
    return pl.pallas_call(
        functools.partial(_decode_kernel, n_steps=n_steps, lam_init=lam_init),
        grid_spec=grid_spec,
        out_shape=jax.ShapeDtypeStruct((nb, dec_seq, ATTN_WIDTH), F32),
        compiler_params=_cparams(("parallel", "arbitrary")),
        name="attn_sample",
    )(page_table.reshape(-1), lam, q2, k_new, v_new, bias_last, bias_new, subln_g,
      *([cache_k] * PPS), *([cache_v] * PPS))


def _ssm_coef_kernel(ldt_ref, ar_ref, ai_ref, bt_re_ref, bt_im_ref, c_re_ref, c_im_ref,
                     kall_ref, s_re_ref, s_im_ref, cp_re_ref, cp_im_ref,
                     abar_re_ref, abar_im_ref, apow_re_ref, apow_im_ref,
                     bb_re_ref, bb_im_ref):
    g = pl.program_id(0)
    dt = jnp.exp(jnp.full((1, SSM_STATE), ldt_ref[g], F32))
    ar = ar_ref[0]
    ai = ai_ref[0]
    mag = jnp.exp(ar * dt)
    abar_re = mag * jnp.cos(ai * dt)
    abar_im = mag * jnp.sin(ai * dt)
    nr, ni = abar_re - 1.0, abar_im
    den = ar * ar + ai * ai
    f_re = (nr * ar + ni * ai) / den
    f_im = (ni * ar - nr * ai) / den
    bt_re, bt_im = bt_re_ref[0], bt_im_ref[0]
    bb_re = f_re * bt_re - f_im * bt_im
    bb_im = f_re * bt_im + f_im * bt_re
    c_re, c_im = c_re_ref[0], c_im_ref[0]

    pw = [(jnp.ones_like(abar_re), jnp.zeros_like(abar_re))]
    for _ in range(SCAN_CHUNK):
        pr, pi = pw[-1]
        pw.append((pr * abar_re - pi * abar_im, pr * abar_im + pi * abar_re))

    cp_re = [c_re * pr - c_im * pi for pr, pi in pw]
    cp_im = [c_re * pi + c_im * pr for pr, pi in pw]
    lhs_re = jnp.concatenate(cp_re[0:SCAN_CHUNK], axis=0)
    lhs_im = jnp.concatenate(cp_im[0:SCAN_CHUNK], axis=0)
    nt = (((1,), (1,)), ((), ()))
    kall_ref[0] = (lax.dot_general(lhs_re, bb_re, nt, precision=_HI, preferred_element_type=F32)
                   - lax.dot_general(lhs_im, bb_im, nt, precision=_HI, preferred_element_type=F32))
    s_re_ref[0] = jnp.concatenate(
        [bb_re * pw[SCAN_CHUNK - 1 - s][0] - bb_im * pw[SCAN_CHUNK - 1 - s][1]
         for s in range(SCAN_CHUNK)], axis=0)
    s_im_ref[0] = jnp.concatenate(
        [bb_re * pw[SCAN_CHUNK - 1 - s][1] + bb_im * pw[SCAN_CHUNK - 1 - s][0]
         for s in range(SCAN_CHUNK)], axis=0)
    cp_re_ref[0] = jnp.concatenate(cp_re[1:SCAN_CHUNK + 1], axis=0)
    cp_im_ref[0] = jnp.concatenate(cp_im[1:SCAN_CHUNK + 1], axis=0)
    abar_re_ref[0] = abar_re
    abar_im_ref[0] = abar_im
    apow_re_ref[0] = pw[SCAN_CHUNK][0]
    apow_im_ref[0] = pw[SCAN_CHUNK][1]
    bb_re_ref[0] = bb_re
    bb_im_ref[0] = bb_im


def _ssm_coefs(log_dt, a_re, a_im, bt_re, bt_im, c_re, c_im):
    g, p, c, lc = N_SSM_GROUPS, SSM_STATE, SSM_GROUP, SCAN_CHUNK
    row = pl.BlockSpec((1, 1, p), lambda i: (i, 0, 0))
    mat = pl.BlockSpec((1, c, p), lambda i: (i, 0, 0))
    big = pl.BlockSpec((1, lc * c, p), lambda i: (i, 0, 0))
    sds = jax.ShapeDtypeStruct
    return pl.pallas_call(
        _ssm_coef_kernel,
        grid=(g,),
        in_specs=[pl.BlockSpec(memory_space=pltpu.SMEM), row, row, mat, mat, mat, mat],
        out_specs=[pl.BlockSpec((1, lc * c, c), lambda i: (i, 0, 0)), big, big, big, big,
                   row, row, row, row, mat, mat],
        out_shape=[sds((g, lc * c, c), F32)] + [sds((g, lc * c, p), F32)] * 4
        + [sds((g, 1, p), F32)] * 4 + [sds((g, c, p), F32)] * 2,
        compiler_params=_cparams(("parallel",)),
        name="ssm_coefs",
    )(log_dt, a_re.reshape(g, 1, p), a_im.reshape(g, 1, p), bt_re, bt_im, c_re, c_im)


def _ssm_local_kernel(u_ref, s_ref, o_ref):
    o_ref[0] = jnp.dot(u_ref[0].astype(BF16), s_ref[0], preferred_element_type=F32)


def _ssm_local(u2, smat):
    gp, nk, w = u2.shape
    n = smat.shape[2]
    return pl.pallas_call(
        _ssm_local_kernel,
        grid=(gp,),
        in_specs=[pl.BlockSpec((1, nk, w), lambda i: (i, 0, 0)),
                  pl.BlockSpec((1, w, n), lambda i: (i, 0, 0))],
        out_specs=pl.BlockSpec((1, nk, n), lambda i: (i, 0, 0)),
        out_shape=jax.ShapeDtypeStruct((gp, nk, n), F32),
        compiler_params=_cparams(("parallel",)),
        name="ssm_local",
    )(u2, smat)


def _ssm_rec_kernel(s_ref, are_ref, aim_ref, h_ref, fin_ref, *, nk):
    w = are_ref.shape[1]
    a_re = are_ref[...]
    a_im = aim_ref[...]

    def body(k, carry):
        hr, hi = carry
        h_ref[k] = jnp.concatenate([hr, hi], axis=1)
        sk = s_ref[k]
        return (a_re * hr - a_im * hi + sk[:, 0:w], a_re * hi + a_im * hr + sk[:, w:2 * w])

    z = jnp.zeros(a_re.shape, F32)
    hr, hi = lax.fori_loop(0, nk, body, (z, z))
    fin_ref[...] = jnp.concatenate([hr, hi], axis=1)


def _ssm_rec(s_kg, apow_re, apow_im):
    nk, gp, w2 = s_kg.shape
    return pl.pallas_call(
        functools.partial(_ssm_rec_kernel, nk=nk),
        out_shape=[jax.ShapeDtypeStruct((nk, gp, w2), F32),
                   jax.ShapeDtypeStruct((gp, w2), F32)],
        compiler_params=pltpu.CompilerParams(vmem_limit_bytes=VMEM_LIMIT),
        name="ssm_rec",
    )(s_kg, apow_re, apow_im)


def _ssm_out_kernel(u_ref, t_ref, h_ref, m_ref, d_ref, o_ref):
    u = u_ref[0]
    y = jnp.dot(u.astype(BF16), t_ref[0], preferred_element_type=F32)
    y = y + jnp.dot(h_ref[0].astype(BF16), m_ref[0], preferred_element_type=F32)
    o_ref[0] = y + d_ref[0] * u


def _ssm_out(u2, tmat, hst, mmat, dtile):
    gp, nk, w = u2.shape
    hw = hst.shape[2]
    return pl.pallas_call(
        _ssm_out_kernel,
        grid=(gp,),
        in_specs=[pl.BlockSpec((1, nk, w), lambda i: (i, 0, 0)),
                  pl.BlockSpec((1, w, w), lambda i: (i, 0, 0)),
                  pl.BlockSpec((1, nk, hw), lambda i: (i, 0, 0)),
                  pl.BlockSpec((1, hw, w), lambda i: (i, 0, 0)),
                  pl.BlockSpec((1, 1, w), lambda i: (i, 0, 0))],
        out_specs=pl.BlockSpec((1, nk, w), lambda i: (i, 0, 0)),
        out_shape=jax.ShapeDtypeStruct((gp, nk, w), F32),
        compiler_params=_cparams(("parallel",)),
        name="ssm_out",
    )(u2, tmat, hst, mmat, dtile)


def _pair_diag(m):
    g, r, c = m.shape
    m = m.reshape(g // 2, 2, r, c)
    z = jnp.zeros((g // 2, r, c), m.dtype)
    top = jnp.concatenate([m[:, 0], z], axis=2)
    bot = jnp.concatenate([z, m[:, 1]], axis=2)
    return jnp.concatenate([top, bot], axis=1)


def _ssm_prompt(u, coefs, ssm_d):
    kall, s_re, s_im, cp_re, cp_im, _, _, apow_re, apow_im, _, _ = coefs
    t = u.shape[0]
    g, p, c, lc = N_SSM_GROUPS, SSM_STATE, SSM_GROUP, SCAN_CHUNK
    nk, gp = t // lc, g // 2
    u2 = u.reshape(nk, lc, gp, 2, c).transpose(2, 0, 3, 1, 4).reshape(gp, nk, 2 * lc * c)
    smat = jnp.concatenate([_pair_diag(s_re), _pair_diag(s_im)], axis=2).astype(BF16)
    s_loc = _ssm_local(u2, smat)
    apr = apow_re.reshape(gp, 2 * p)
    api = apow_im.reshape(gp, 2 * p)
    hst, fin = _ssm_rec(s_loc.transpose(1, 0, 2), apr, api)
    k4 = kall.reshape(g, lc, c, c)
    idx = np.arange(lc)[None, :] - np.arange(lc)[:, None]
    tg = jnp.where((idx >= 0)[None, :, :, None, None], k4[:, np.maximum(idx, 0)], 0.0)
    tg = tg.transpose(0, 1, 4, 2, 3).reshape(g, lc * c, lc * c)
    tmat = _pair_diag(tg).astype(BF16)
    mmat = jnp.concatenate([_pair_diag(cp_re.transpose(0, 2, 1)),
                            _pair_diag(-cp_im.transpose(0, 2, 1))], axis=1).astype(BF16)
    dtile = jnp.tile(ssm_d.reshape(gp, 2, 1, c), (1, 1, lc, 1)).reshape(gp, 1, 2 * lc * c)
    y2 = _ssm_out(u2, tmat, hst.transpose(1, 0, 2), mmat, dtile)
    y = y2.reshape(gp, nk, 2, lc, c).transpose(1, 3, 0, 2, 4).reshape(t, g * c)
    fin = fin.reshape(gp, 2, 2, p)
    return y, fin[:, 0].reshape(g, p), fin[:, 1].reshape(g, p)


def _ssm_sample_kernel(u_ref, hr_ref, hi_ref, are_ref, aim_ref, bbr_ref, bbi_ref,
                       cr_ref, ci_ref, d_ref, y_ref, fr_ref, fi_ref, *, nb, steps):
    u = u_ref[0]
    nt = (((1,), (0,)), ((), ()))
    bu_re = lax.dot_general(u, bbr_ref[0], nt, precision=_HI, preferred_element_type=F32)
    bu_im = lax.dot_general(u, bbi_ref[0], nt, precision=_HI, preferred_element_type=F32)
    a_re, a_im = are_ref[0], aim_ref[0]
    hr, hi = hr_ref[0], hi_ref[0]
    hs_re, hs_im = [], []
    for l in range(steps):
        sl = slice(l * nb, (l + 1) * nb)
        hr, hi = (a_re * hr - a_im * hi + bu_re[sl], a_re * hi + a_im * hr + bu_im[sl])
        hs_re.append(hr)
        hs_im.append(hi)
    h_re = jnp.concatenate(hs_re, axis=0).astype(BF16)
    h_im = jnp.concatenate(hs_im, axis=0).astype(BF16)
    ntt = (((1,), (1,)), ((), ()))
    y = (lax.dot_general(h_re, cr_ref[0].astype(BF16), ntt, preferred_element_type=F32)
         - lax.dot_general(h_im, ci_ref[0].astype(BF16), ntt, preferred_element_type=F32))
    y_ref[0] = y + d_ref[0] * u
    fr_ref[0] = hr
    fi_ref[0] = hi


def _ssm_sample(u_g, h0_re, h0_im, coefs, c_re, c_im, ssm_d, nb, steps):
    abar_re, abar_im = coefs[5], coefs[6]
    bb_re, bb_im = coefs[9], coefs[10]
    g, p, c = N_SSM_GROUPS, SSM_STATE, SSM_GROUP
    rows = steps * nb
    sds = jax.ShapeDtypeStruct
    spec = lambda a, b: pl.BlockSpec((1, a, b), lambda i: (i, 0, 0))
    return pl.pallas_call(
        functools.partial(_ssm_sample_kernel, nb=nb, steps=steps),
        grid=(g,),
        in_specs=[spec(rows, c), spec(nb, p), spec(nb, p), spec(1, p), spec(1, p),
                  spec(c, p), spec(c, p), spec(c, p), spec(c, p), spec(1, c)],
        out_specs=[spec(rows, c), spec(nb, p), spec(nb, p)],
        out_shape=[sds((g, rows, c), F32), sds((g, nb, p), F32), sds((g, nb, p), F32)],
        compiler_params=_cparams(("parallel",)),
        name="ssm_sample",
    )(u_g, h0_re, h0_im, abar_re, abar_im, bb_re, bb_im, c_re, c_im,
      ssm_d.reshape(g, 1, c))


def _layer_norm(x, g, b):
    mu = jnp.mean(x, axis=-1, keepdims=True)
    var = jnp.mean(jnp.square(x - mu), axis=-1, keepdims=True)
    return (x - mu) * lax.rsqrt(var + LN_EPS) * g + b


def _merge_kernel(o_ref, y_ref, x_ref, gw_ref, gb_ref, wo_ref, g_ref, b_ref,
                  x1_ref, x1t_ref):
    z = jax.nn.gelu(y_ref[...])
    gate = jnp.dot(z.astype(BF16), gw_ref[...], preferred_element_type=F32) + gb_ref[...]
    z = z * jax.nn.sigmoid(gate)
    mix = jnp.dot(o_ref[...].astype(BF16), wo_ref[0:ATTN_WIDTH, :], preferred_element_type=F32)
    mix = mix + jnp.dot(z.astype(BF16), wo_ref[ATTN_WIDTH:, :], preferred_element_type=F32)
    x1 = _layer_norm(DEEPNORM_ALPHA * x_ref[...] + mix, g_ref[...], b_ref[...])
    x1_ref[...] = x1
    x1t_ref[...] = x1.T.astype(BF16)


def _merge(attn_o, y_ssm, x, glu_w, glu_b, w_out, ln_g, ln_b, tm):
    t = x.shape[0]
    full = lambda r, c: pl.BlockSpec((r, c), lambda i: (0, 0))
    return pl.pallas_call(
        _merge_kernel,
        grid=(t // tm,),
        in_specs=[pl.BlockSpec((tm, ATTN_WIDTH), lambda i: (i, 0)),
                  pl.BlockSpec((tm, SSM_WIDTH), lambda i: (i, 0)),
                  pl.BlockSpec((tm, D_MODEL), lambda i: (i, 0)),
                  full(SSM_WIDTH, SSM_WIDTH), full(1, SSM_WIDTH),
                  full(D_MODEL, D_MODEL), full(1, D_MODEL), full(1, D_MODEL)],
        out_specs=[pl.BlockSpec((tm, D_MODEL), lambda i: (i, 0)),
                   pl.BlockSpec((D_MODEL, tm), lambda i: (0, i))],
        out_shape=[jax.ShapeDtypeStruct((t, D_MODEL), F32),
                   jax.ShapeDtypeStruct((D_MODEL, t), BF16)],
        compiler_params=_cparams(("parallel",)),
        name="merge_out_ln1",
    )(attn_o, y_ssm, x, glu_w, glu_b, w_out, ln_g, ln_b)


def _top_desc(vals, n):
    rows = []
    for r in range(n):
        mx = jnp.max(vals, axis=0, keepdims=True)
        rows.append(mx)
        if r + 1 < n:
            vals = jnp.where(vals == mx, -jnp.inf, vals)
    return rows


def _route_kernel(xt_ref, wq_ref, keys_ref, s1_ref, s2_ref, e1_ref, e2_ref, thr_ref):
    qt = jnp.dot(wq_ref[...], xt_ref[...], preferred_element_type=F32).astype(BF16)
    thr_rows = []
    for h in range(PEER_HEADS):
        tops = []
        for m in range(2):
            hm = 2 * h + m
            sc = jnp.dot(keys_ref[hm], qt[hm * PEER_NKEYS:(hm + 1) * PEER_NKEYS, :],
                         preferred_element_type=F32)
            tops.append((sc, _top_desc(sc, PEER_TOPK)))
        (sc1, a), (sc2, b) = tops
        bmat = jnp.concatenate(b, axis=0)
        cand = jnp.concatenate([a[r] + bmat for r in range(PEER_TOPK)], axis=0)
        best = _top_desc(cand, PEER_TOPK)
        zsum = jnp.ones_like(best[0])
        for r in range(1, PEER_TOPK):
            zsum = zsum + jnp.exp(best[r] - best[0])
        s1_ref[h] = sc1
        s2_ref[h] = sc2
        e1_ref[h] = jnp.exp(sc1 - a[0]) / zsum
        e2_ref[h] = jnp.exp(sc2 - b[0])
        thr_rows.append(best[PEER_TOPK - 1])
    thr_ref[...] = jnp.concatenate(thr_rows, axis=0)


def _route(x1t, wq_t, keys, tm):
    t = x1t.shape[1]
    big = pl.BlockSpec((PEER_HEADS, PEER_NKEYS, tm), lambda i: (0, 0, i))
    sds = jax.ShapeDtypeStruct((PEER_HEADS, PEER_NKEYS, t), F32)
    return pl.pallas_call(
        _route_kernel,
        grid=(t // tm,),
        in_specs=[pl.BlockSpec((D_MODEL, tm), lambda i: (0, i)),
                  pl.BlockSpec(wq_t.shape, lambda i: (0, 0)),
                  pl.BlockSpec(keys.shape, lambda i: (0, 0, 0))],
        out_specs=[big, big, big, big, pl.BlockSpec((PEER_HEADS, tm), lambda i: (0, i))],
        out_shape=[sds, sds, sds, sds, jax.ShapeDtypeStruct((PEER_HEADS, t), F32)],
        compiler_params=_cparams(("parallel",)),
        name="peer_route",
    )(x1t, wq_t, keys)


IPB = 4


def _peer_kernel(xt_ref, u_ref, vt_ref, s1_ref, s2_ref, e1_ref, e2_ref, thr_ref,
                 o_ref, a_sc):
    eb = pl.program_id(1)
    ht = jnp.dot(u_ref[...], xt_ref[...], preferred_element_type=F32)
    for ii in range(IPB):
        i = eb * IPB + ii
        w = jnp.zeros((PEER_NKEYS, ht.shape[1]), F32)
        for h in range(PEER_HEADS):
            s1row = s1_ref[h, pl.ds(i, 1), :]
            e1row = e1_ref[h, pl.ds(i, 1), :]
            pair = s2_ref[h] + s1row
            w = w + jnp.where(pair >= thr_ref[h:h + 1, :], e2_ref[h] * e1row, 0.0)
        rows = slice(ii * PEER_NKEYS, (ii + 1) * PEER_NKEYS)
        a_sc[rows, :] = (w * jax.nn.gelu(ht[rows, :])).astype(BF16)
    part = jnp.dot(vt_ref[...], a_sc[...], preferred_element_type=F32)

    @pl.when(eb == 0)
    def _first():
        o_ref[...] = part

    @pl.when(eb > 0)
    def _rest():
        o_ref[...] += part


def _peer(x1t, u_tab, vt_tab, s1, s2, e1, e2, thr, tm):
    t = x1t.shape[1]
    te = IPB * PEER_NKEYS
    n_eb = u_tab.shape[0] // te
    big = pl.BlockSpec((PEER_HEADS, PEER_NKEYS, tm), lambda i, e: (0, 0, i))
    return pl.pallas_call(
        _peer_kernel,
        grid=(t // tm, n_eb),
        in_specs=[pl.BlockSpec((D_MODEL, tm), lambda i, e: (0, i)),
                  pl.BlockSpec((te, D_MODEL), lambda i, e: (e, 0)),
                  pl.BlockSpec((D_MODEL, te), lambda i, e: (0, e)),
                  big, big, big, big,
                  pl.BlockSpec((PEER_HEADS, tm), lambda i, e: (0, i))],
        out_specs=pl.BlockSpec((D_MODEL, tm), lambda i, e: (0, i)),
        out_shape=jax.ShapeDtypeStruct((D_MODEL, t), F32),
        scratch_shapes=[pltpu.VMEM((te, tm), BF16)],
        compiler_params=_cparams(("parallel", "arbitrary")),
        name="peer_experts",
    )(x1t, u_tab, vt_tab, s1, s2, e1, e2, thr)


def _ln2_kernel(x1_ref, ft_ref, g_ref, b_ref, o_ref):
    o_ref[...] = _layer_norm(DEEPNORM_ALPHA * x1_ref[...] + ft_ref[...].T,
                             g_ref[...], b_ref[...])


def _ln2(x1, ffn_t, ln_g, ln_b, tm):
    t = x1.shape[0]
    return pl.pallas_call(
        _ln2_kernel,
        grid=(t // tm,),
        in_specs=[pl.BlockSpec((tm, D_MODEL), lambda i: (i, 0)),
                  pl.BlockSpec((D_MODEL, tm), lambda i: (0, i)),
                  pl.BlockSpec((1, D_MODEL), lambda i: (0, 0)),
                  pl.BlockSpec((1, D_MODEL), lambda i: (0, 0))],
        out_specs=pl.BlockSpec((tm, D_MODEL), lambda i: (i, 0)),
        out_shape=jax.ShapeDtypeStruct((t, D_MODEL), F32),
        compiler_params=_cparams(("parallel",)),
        name="ln2",
    )(x1, ffn_t, ln_g, ln_b)


def _prompt_buckets(tb):
    r = np.arange(tb)[:, None]
    c = np.arange(tb)[None, :]
    diag = np.where(c <= r, _bucket_np(r - c), -1)
    sub = _bucket_np(r - c + tb)
    return np.stack([diag, sub]).astype(np.int32)


def _sample_buckets(past_len, dec_seq):
    qpos = past_len + np.tile(np.arange(dec_seq), 2)[:, None]
    w = PPS * PAGE_SIZE
    kpos = (past_len - w) + np.arange(w)[None, :]
    last = _bucket_np(qpos - kpos)
    jn = np.arange(NEWK)[None, :]
    knew = past_len + jn
    new = np.where((jn < dec_seq) & (knew <= qpos), _bucket_np(qpos - knew), -1)
    return last[None].astype(np.int32), new[None].astype(np.int32)


def _group_tail(x1, x1t, w_query_t, keys, u_tab, vt_tab, ln_g, ln_b, tm_route, tm_peer, tm_ln):
    s1, s2, e1, e2, thr = _route(x1t, w_query_t, keys, tm_route)
    ffn_t = _peer(x1t, u_tab, vt_tab, s1, s2, e1, e2, thr, tm_peer)
    return _ln2(x1, ffn_t, ln_g, ln_b, tm_ln)


def kernel(x_prompt, x_sample, cache_k, cache_v, state_ssm_re, state_ssm_im, page_table, w_in, lambda_q1, lambda_k1, lambda_q2, lambda_k2, attn_subln_g, rel_bias, ssm_a_re, ssm_a_im, ssm_b_re, ssm_b_im, ssm_c_re, ssm_c_im, ssm_d, ssm_log_dt, ssm_glu_w, ssm_glu_b, w_out, ln1_g, ln1_b, peer_w_query, peer_sub_keys, peer_u, peer_v, ln2_g, ln2_b):
    l = 0
    seq = x_prompt.shape[1]
    dec_b, dec_s = x_sample.shape[0], x_sample.shape[1]
    n_pages = page_table.shape[1]
    past_len = n_pages * PAGE_SIZE
    g, p, c = N_SSM_GROUPS, SSM_STATE, SSM_GROUP
    lam_init = 0.8 - 0.6 * math.exp(-0.3 * l)
    lam = (jnp.exp(jnp.sum(lambda_q1[l] * lambda_k1[l]))
           - jnp.exp(jnp.sum(lambda_q2[l] * lambda_k2[l])) + lam_init).reshape(1).astype(F32)

    w_in_b = w_in[l].astype(BF16)
    glu_w_b = ssm_glu_w[l].astype(BF16)
    w_out_b = w_out[l].astype(BF16)
    wq_t = peer_w_query[l].T.astype(BF16)
    keys = peer_sub_keys[l].reshape(2 * PEER_HEADS, PEER_NKEYS, PEER_NKEYS).astype(BF16)
    u_tab = peer_u[l].astype(BF16)
    vt_tab = peer_v[l].T.astype(BF16)
    subln_g = attn_subln_g[l].reshape(1, HEAD_DIM)
    glu_b = ssm_glu_b[l].reshape(1, SSM_WIDTH)
    g1, b1 = ln1_g[l].reshape(1, D_MODEL), ln1_b[l].reshape(1, D_MODEL)
    g2, b2 = ln2_g[l].reshape(1, D_MODEL), ln2_b[l].reshape(1, D_MODEL)
    coefs = _ssm_coefs(ssm_log_dt[l], ssm_a_re[l], ssm_a_im[l],
                       ssm_b_re[l].transpose(0, 2, 1), ssm_b_im[l].transpose(0, 2, 1),
                       ssm_c_re[l], ssm_c_im[l])

    xp = x_prompt.reshape(seq, D_MODEL)
    tb = 512
    qkvu = _proj(xp, w_in_b, 512, ATTN_WIDTH)
    bias_p = _bias_tiles(rel_bias, _prompt_buckets(tb))
    attn_p = _attn_prompt(qkvu, bias_p, lam, subln_g, tb, lam_init)
    y_p, hr_p, hi_p = _ssm_prompt(qkvu[3], coefs, ssm_d[l])
    x1_p, x1t_p = _merge(attn_p, y_p, xp, glu_w_b, glu_b, w_out_b, g1, b1, 256)
    out_p = _group_tail(x1_p, x1t_p, wq_t, keys, u_tab, vt_tab, g2, b2, 256, 512, 256)

    ts = dec_b * dec_s
    xs = x_sample.reshape(ts, D_MODEL)
    qkvu_s = _proj(xs, w_in_b, ts, ATTN_WIDTH)
    q_s = qkvu_s[0].reshape(dec_b, dec_s, N_HEADS, HEAD_DIM) * (HALF_DIM ** -0.5)
    lane = np.arange(HEAD_DIM)
    halves = jnp.asarray(np.stack([lane < HALF_DIM, lane >= HALF_DIM]).astype(np.float32))
    q2 = (q_s[:, None] * halves[None, :, None, None, :])
    q2 = q2.transpose(0, 3, 1, 2, 4).reshape(dec_b, N_HEADS, QROWS, HEAD_DIM)
    pad_new = lambda a: jnp.pad(
        a.reshape(dec_b, dec_s, N_HEADS, HEAD_DIM).transpose(0, 2, 1, 3),
        ((0, 0), (0, 0), (0, NEWK - dec_s), (0, 0)))
    bl, bn = _sample_buckets(past_len, dec_s)
    bias_last = _bias_tiles(rel_bias, bl)[:, 0]
    bias_new = _bias_tiles(rel_bias, bn)[:, 0]
    attn_s = _attn_sample(q2, pad_new(qkvu_s[1]), pad_new(qkvu_s[2]), cache_k, cache_v,
                          page_table, bias_last, bias_new, lam, subln_g, lam_init)
    u_s = qkvu_s[3].reshape(dec_b, dec_s, g, c).transpose(2, 1, 0, 3).reshape(g, ts, c)
    y_g, hr_s, hi_s = _ssm_sample(u_s, state_ssm_re[l].transpose(1, 0, 2),
                                  state_ssm_im[l].transpose(1, 0, 2), coefs,
                                  ssm_c_re[l], ssm_c_im[l], ssm_d[l], dec_b, dec_s)
    y_s = y_g.reshape(g, dec_s, dec_b, c).transpose(2, 1, 0, 3).reshape(ts, SSM_WIDTH)
    x1_s, x1t_s = _merge(attn_s.reshape(ts, ATTN_WIDTH), y_s, xs, glu_w_b, glu_b, w_out_b,
                         g1, b1, ts)
    out_s = _group_tail(x1_s, x1t_s, wq_t, keys, u_tab, vt_tab, g2, b2, ts, ts, ts)

    kv = lambda a, b_, s_: a.reshape(1, b_, s_, N_HEADS, HEAD_DIM)
    return (out_p.reshape(1, seq, D_MODEL), out_s.reshape(dec_b, dec_s, D_MODEL),
            kv(qkvu[1], 1, seq), kv(qkvu[2], 1, seq),
            hr_p.reshape(1, 1, g, p), hi_p.reshape(1, 1, g, p),
            kv(qkvu_s[1], dec_b, dec_s), kv(qkvu_s[2], dec_b, dec_s),
            hr_s.transpose(1, 0, 2)[None], hi_s.transpose(1, 0, 2)[None])
```

```python
import functools
import math

import numpy as np
import jax
import jax.numpy as jnp
from jax import lax
from jax.experimental import pallas as pl
from jax.experimental.pallas import tpu as pltpu

F32 = jnp.float32
BF16 = jnp.bfloat16

D_MODEL = 2048
PAGE_SIZE = 128
ATTN_WIDTH = 1024
SSM_WIDTH = 1024
N_HEADS = 8
HEAD_DIM = 128
HALF_DIM = 64
N_BUCKETS = 32
MAX_DISTANCE = 128
SSM_GROUP = 16
N_SSM_GROUPS = 64
SSM_STATE = 64
PEER_HEADS = 8
PEER_NKEYS = 128
PEER_TOPK = 16
LN_EPS = 1e-5
DEPTH = 1
DEEPNORM_ALPHA = (2.0 * DEPTH) ** 0.25

LANES = 128
SCAN_CHUNK = 16
NEG = -1e30
VMEM_LIMIT = 56 * 1024 * 1024

_HI = lax.Precision.HIGHEST


def _cparams(sem):
    return pltpu.CompilerParams(dimension_semantics=sem, vmem_limit_bytes=VMEM_LIMIT)


def _rep(x, n):
    return x if n == 1 else jnp.concatenate([x] * n, axis=1)


def _proj_kernel(x_ref, w_ref, o_ref):
    o_ref[0] = jnp.dot(x_ref[...].astype(BF16), w_ref[...], preferred_element_type=F32)


def _proj(x, w, tm, tn):
    t, k = x.shape
    n = w.shape[1]
    return pl.pallas_call(
        _proj_kernel,
        grid=(t // tm, n // tn),
        in_specs=[pl.BlockSpec((tm, k), lambda i, j: (i, 0)),
                  pl.BlockSpec((k, tn), lambda i, j: (0, j))],
        out_specs=pl.BlockSpec((1, tm, tn), lambda i, j: (j, i, 0)),
        out_shape=jax.ShapeDtypeStruct((n // tn, t, tn), F32),
        compiler_params=_cparams(("parallel", "arbitrary")),
        name="in_proj",
    )(x, w)


def _bucket_np(d):
    n = np.maximum(d, 0)
    max_exact = N_BUCKETS // 2
    nf = np.maximum(n, 1).astype(np.float64)
    large = max_exact + (np.log(nf / max_exact) / math.log(MAX_DISTANCE / max_exact)
                         * (N_BUCKETS - max_exact)).astype(np.int32)
    large = np.minimum(large, N_BUCKETS - 1)
    return np.where(n < max_exact, n, large).astype(np.int32)


def _bias_kernel(rb_ref, bkt_ref, o_ref):
    h = pl.program_id(0)
    bkt = bkt_ref[0]
    far = rb_ref[N_BUCKETS - 1, h]
    acc = jnp.where(bkt < 0, NEG, 0.0).astype(F32)
    for b in range(N_BUCKETS - 1):
        acc = jnp.where(bkt == b, rb_ref[b, h] - far, acc)
    o_ref[0, 0] = acc


def _bias_tiles(rel_bias, bkt):
    n, r, c = bkt.shape
    return pl.pallas_call(
        _bias_kernel,
        grid=(N_HEADS, n),
        in_specs=[pl.BlockSpec(memory_space=pltpu.SMEM),
                  pl.BlockSpec((1, r, c), lambda h, i: (i, 0, 0))],
        out_specs=pl.BlockSpec((1, 1, r, c), lambda h, i: (h, i, 0, 0)),
        out_shape=jax.ShapeDtypeStruct((N_HEADS, n, r, c), F32),
        compiler_params=_cparams(("parallel", "arbitrary")),
        name="bias_tiles",
    )(rel_bias, jnp.asarray(bkt))


def _subln(attn, g_row, lam_init):
    ms = jnp.mean(jnp.square(attn), axis=-1, keepdims=True)
    return attn * lax.rsqrt(ms + LN_EPS) * g_row * (1.0 - lam_init)


def _attn_kernel(qi_ref, ki_ref, lam_ref, q_ref, k_ref, v_ref, bias_ref, g_ref, o_ref,
                 q2_sc, m_sc, l_sc, acc_sc, *, tb, lam_init):
    s = pl.program_id(1)
    qi = qi_ref[s]
    ki = ki_ref[s]

    @pl.when(ki == 0)
    def _init():
        q = q_ref[0] * (HALF_DIM ** -0.5)
        lane = lax.broadcasted_iota(jnp.int32, q.shape, 1)
        q2_sc[0:tb, :] = jnp.where(lane < HALF_DIM, q, 0.0).astype(BF16)
        q2_sc[tb:2 * tb, :] = jnp.where(lane >= HALF_DIM, q, 0.0).astype(BF16)
        m_sc[...] = jnp.full(m_sc.shape, NEG, F32)
        l_sc[...] = jnp.zeros(l_sc.shape, F32)
        acc_sc[...] = jnp.zeros(acc_sc.shape, F32)

    def logits():
        kb = k_ref[0].astype(BF16)
        return lax.dot_general(q2_sc[...], kb, (((1,), (1,)), ((), ())),
                               preferred_element_type=F32)

    def update(sc):
        m_prev = m_sc[...]
        m_new = jnp.maximum(m_prev, jnp.max(sc, axis=-1, keepdims=True))
        alpha = jnp.exp(m_prev - m_new)
        p = jnp.exp(sc - _rep(m_new, tb // LANES))
        l_sc[...] = alpha * l_sc[...] + jnp.sum(p, axis=-1, keepdims=True)
        acc_sc[...] = alpha * acc_sc[...] + jnp.dot(
            p.astype(BF16), v_ref[0].astype(BF16), preferred_element_type=F32)
        m_sc[...] = m_new

    @pl.when(ki < qi - 1)
    def _far():
        update(logits())

    @pl.when(ki == qi - 1)
    def _sub():
        b = bias_ref[0, 1]
        update(logits() + jnp.concatenate([b, b], axis=0))

    @pl.when(ki == qi)
    def _diag():
        b = bias_ref[0, 0]
        update(logits() + jnp.concatenate([b, b], axis=0))
        o = acc_sc[...] / l_sc[...]
        attn = o[0:tb] - lam_ref[0] * o[tb:2 * tb]
        o_ref[...] = _subln(attn, g_ref[...], lam_init)


def _attn_prompt(qkvu, bias, lam, subln_g, tb, lam_init):
    t = qkvu.shape[1]
    nb = t // tb
    qi_idx = np.concatenate([np.full(i + 1, i) for i in range(nb)]).astype(np.int32)
    ki_idx = np.concatenate([np.arange(i + 1) for i in range(nb)]).astype(np.int32)
    grid_spec = pltpu.PrefetchScalarGridSpec(
        num_scalar_prefetch=2,
        grid=(N_HEADS, len(qi_idx)),
        in_specs=[
            pl.BlockSpec(memory_space=pltpu.SMEM),
            pl.BlockSpec((1, tb, HEAD_DIM), lambda h, s, qi, ki: (0, qi[s], h)),
            pl.BlockSpec((1, tb, HEAD_DIM), lambda h, s, qi, ki: (1, ki[s], h)),
            pl.BlockSpec((1, tb, HEAD_DIM), lambda h, s, qi, ki: (2, ki[s], h)),
            pl.BlockSpec((1, 2, tb, tb), lambda h, s, qi, ki: (h, 0, 0, 0)),
            pl.BlockSpec((1, HEAD_DIM), lambda h, s, qi, ki: (0, 0)),
        ],
        out_specs=pl.BlockSpec((tb, HEAD_DIM), lambda h, s, qi, ki: (qi[s], h)),
        scratch_shapes=[
            pltpu.VMEM((2 * tb, HEAD_DIM), BF16),
            pltpu.VMEM((2 * tb, LANES), F32),
            pltpu.VMEM((2 * tb, LANES), F32),
            pltpu.VMEM((2 * tb, HEAD_DIM), F32),
        ],
    )
    return pl.pallas_call(
        functools.partial(_attn_kernel, tb=tb, lam_init=lam_init),
        grid_spec=grid_spec,
        out_shape=jax.ShapeDtypeStruct((t, ATTN_WIDTH), F32),
        compiler_params=_cparams(("parallel", "arbitrary")),
        name="attn_prompt",
    )(jnp.asarray(qi_idx), jnp.asarray(ki_idx), lam, qkvu, qkvu, qkvu, bias, subln_g)


PPS = 8
QROWS = 8
NEWK = 16


def _decode_kernel(pt_ref, lam_ref, q2_ref, kn_ref, vn_ref, bl_ref, bn_ref, g_ref, *rest,
                   n_steps, lam_init):
    k_refs = rest[0:PPS]
    v_refs = rest[PPS:2 * PPS]
    o_ref = rest[2 * PPS]
    m_sc, l_sc, acc_sc = rest[2 * PPS + 1:]
    p_idx = pl.program_id(1)
    last = p_idx == n_steps - 1

    @pl.when(p_idx == 0)
    def _init():
        m_sc[...] = jnp.full(m_sc.shape, NEG, F32)
        l_sc[...] = jnp.zeros(l_sc.shape, F32)
        acc_sc[...] = jnp.zeros(acc_sc.shape, F32)

    last_f = jnp.where(last, 1.0, 0.0).astype(F32)

    def update(h, sc, vmat):
        m_prev = m_sc[h]
        m_new = jnp.maximum(m_prev, jnp.max(sc, axis=-1, keepdims=True))
        alpha = jnp.exp(m_prev - m_new)
        p = jnp.exp(sc - _rep(m_new, sc.shape[1] // LANES) if sc.shape[1] >= LANES
                    else sc - m_new[:, 0:sc.shape[1]])
        l_sc[h] = alpha * l_sc[h] + jnp.sum(p, axis=-1, keepdims=True)
        acc_sc[h] = alpha * acc_sc[h] + jnp.dot(p.astype(BF16), vmat,
                                                 preferred_element_type=F32)
        m_sc[h] = m_new

    for h in range(N_HEADS):
        q2 = q2_ref[0, h].astype(BF16)
        rows_h = pl.ds(h, PAGE_SIZE, stride=N_HEADS)
        kcat = jnp.concatenate([r[0, rows_h, :] for r in k_refs], axis=0).astype(BF16)
        vcat = jnp.concatenate([r[0, rows_h, :] for r in v_refs], axis=0).astype(BF16)
        sc = lax.dot_general(q2, kcat, (((1,), (1,)), ((), ())), preferred_element_type=F32)
        sc = sc + bl_ref[h] * last_f
        update(h, sc, vcat)

    @pl.when(last)
    def _finish():
        for h in range(N_HEADS):
            q2 = q2_ref[0, h].astype(BF16)
            kn = kn_ref[0, h].astype(BF16)
            vn = vn_ref[0, h].astype(BF16)
            sc = lax.dot_general(q2, kn, (((1,), (1,)), ((), ())), preferred_element_type=F32)
            update(h, sc + bn_ref[h], vn)
            o = acc_sc[h] / l_sc[h]
            half = QROWS // 2
            attn = o[0:half] - lam_ref[0] * o[half:QROWS]
            o_ref[0, :, h * HEAD_DIM:(h + 1) * HEAD_DIM] = _subln(attn, g_ref[...], lam_init)


def _attn_sample(q2, k_new, v_new, cache_k, cache_v, page_table, bias_last, bias_new,
                 lam, subln_g, lam_init):
    nb, n_pages = page_table.shape
    n_steps = n_pages // PPS
    dec_seq = QROWS // 2
    pages = lambda cache: cache.reshape(cache.shape[0], PAGE_SIZE * N_HEADS, HEAD_DIM)

    def page_spec(j):
        return pl.BlockSpec(
            (1, PAGE_SIZE * N_HEADS, HEAD_DIM),
            lambda b, p, pt, j=j: (pt[b * n_pages + p * PPS + j], 0, 0))

    grid_spec = pltpu.PrefetchScalarGridSpec(
        num_scalar_prefetch=1,
        grid=(nb, n_steps),
        in_specs=[
            pl.BlockSpec(memory_space=pltpu.SMEM),
            pl.BlockSpec((1, N_HEADS, QROWS, HEAD_DIM), lambda b, p, pt: (b, 0, 0, 0)),
            pl.BlockSpec((1, N_HEADS, NEWK, HEAD_DIM), lambda b, p, pt: (b, 0, 0, 0)),
            pl.BlockSpec((1, N_HEADS, NEWK, HEAD_DIM), lambda b, p, pt: (b, 0, 0, 0)),
            pl.BlockSpec((N_HEADS, QROWS, PPS * PAGE_SIZE), lambda b, p, pt: (0, 0, 0)),
            pl.BlockSpec((N_HEADS, QROWS, NEWK), lambda b, p, pt: (0, 0, 0)),
            pl.BlockSpec((1, HEAD_DIM), lambda b, p, pt: (0, 0)),
        ] + [page_spec(j) for j in range(PPS)] + [page_spec(j) for j in range(PPS)],
        out_specs=pl.BlockSpec((1, dec_seq, ATTN_WIDTH), lambda b, p, pt: (b, 0, 0)),
        scratch_shapes=[
            pltpu.VMEM((N_HEADS, QROWS, LANES), F32),
            pltpu.VMEM((N_HEADS, QROWS, LANES), F32),
            pltpu.VMEM((N_HEADS, QROWS, HEAD_DIM), F32),
        ],
    )
    return pl.pallas_call(
        functools.partial(_decode_kernel, n_steps=n_steps, lam_init=lam_init),
        grid_spec=grid_spec,
        out_shape=jax.ShapeDtypeStruct((nb, dec_seq, ATTN_WIDTH), F32),
        compiler_params=_cparams(("parallel", "arbitrary")),
        name="attn_sample",
    )(page_table.reshape(-1), lam, q2, k_new, v_new, bias_last, bias_new, subln_g,
      *([pages(cache_k)] * PPS), *([pages(cache_v)] * PPS))


def _ssm_coef_kernel(ldt_ref, ar_ref, ai_ref, bt_re_ref, bt_im_ref, c_re_ref, c_im_ref,
                     kall_ref, s_re_ref, s_im_ref, cp_re_ref, cp_im_ref,
                     abar_re_ref, abar_im_ref, apow_re_ref, apow_im_ref,
                     bb_re_ref, bb_im_ref):
    g = pl.program_id(0)
    dt = jnp.exp(jnp.full((1, SSM_STATE), ldt_ref[g], F32))
    ar = ar_ref[0]
    ai = ai_ref[0]
    mag = jnp.exp(ar * dt)
    abar_re = mag * jnp.cos(ai * dt)
    abar_im = mag * jnp.sin(ai * dt)
    nr, ni = abar_re - 1.0, abar_im
    den = ar * ar + ai * ai
    f_re = (nr * ar + ni * ai) / den
    f_im = (ni * ar - nr * ai) / den
    bt_re, bt_im = bt_re_ref[0], bt_im_ref[0]
    bb_re = f_re * bt_re - f_im * bt_im
    bb_im = f_re * bt_im + f_im * bt_re
    c_re, c_im = c_re_ref[0], c_im_ref[0]

    pw = [(jnp.ones_like(abar_re), jnp.zeros_like(abar_re))]
    for _ in range(SCAN_CHUNK):
        pr, pi = pw[-1]
        pw.append((pr * abar_re - pi * abar_im, pr * abar_im + pi * abar_re))

    cp_re = [c_re * pr - c_im * pi for pr, pi in pw]
    cp_im = [c_re * pi + c_im * pr for pr, pi in pw]
    lhs_re = jnp.concatenate(cp_re[0:SCAN_CHUNK], axis=0)
    lhs_im = jnp.concatenate(cp_im[0:SCAN_CHUNK], axis=0)
    nt = (((1,), (1,)), ((), ()))
    kall_ref[0] = (lax.dot_general(lhs_re, bb_re, nt, precision=_HI, preferred_element_type=F32)
                   - lax.dot_general(lhs_im, bb_im, nt, precision=_HI, preferred_element_type=F32))
    s_re_ref[0] = jnp.concatenate(
        [bb_re * pw[SCAN_CHUNK - 1 - s][0] - bb_im * pw[SCAN_CHUNK - 1 - s][1]
         for s in range(SCAN_CHUNK)], axis=0)
    s_im_ref[0] = jnp.concatenate(
        [bb_re * pw[SCAN_CHUNK - 1 - s][1] + bb_im * pw[SCAN_CHUNK - 1 - s][0]
         for s in range(SCAN_CHUNK)], axis=0)
    cp_re_ref[0] = jnp.concatenate(cp_re[1:SCAN_CHUNK + 1], axis=0)
    cp_im_ref[0] = jnp.concatenate(cp_im[1:SCAN_CHUNK + 1], axis=0)
    abar_re_ref[0] = abar_re
    abar_im_ref[0] = abar_im
    apow_re_ref[0] = pw[SCAN_CHUNK][0]
    apow_im_ref[0] = pw[SCAN_CHUNK][1]
    bb_re_ref[0] = bb_re
    bb_im_ref[0] = bb_im


def _ssm_coefs(log_dt, a_re, a_im, bt_re, bt_im, c_re, c_im):
    g, p, c, lc = N_SSM_GROUPS, SSM_STATE, SSM_GROUP, SCAN_CHUNK
    row = pl.BlockSpec((1, 1, p), lambda i: (i, 0, 0))
    mat = pl.BlockSpec((1, c, p), lambda i: (i, 0, 0))
    big = pl.BlockSpec((1, lc * c, p), lambda i: (i, 0, 0))
    sds = jax.ShapeDtypeStruct
    return pl.pallas_call(
        _ssm_coef_kernel,
        grid=(g,),
        in_specs=[pl.BlockSpec(memory_space=pltpu.SMEM), row, row, mat, mat, mat, mat],
        out_specs=[pl.BlockSpec((1, lc * c, c), lambda i: (i, 0, 0)), big, big, big, big,
                   row, row, row, row, mat, mat],
        out_shape=[sds((g, lc * c, c), F32)] + [sds((g, lc * c, p), F32)] * 4
        + [sds((g, 1, p), F32)] * 4 + [sds((g, c, p), F32)] * 2,
        compiler_params=_cparams(("parallel",)),
        name="ssm_coefs",
    )(log_dt, a_re.reshape(g, 1, p), a_im.reshape(g, 1, p), bt_re, bt_im, c_re, c_im)


def _ssm_local_kernel(u_ref, s_ref, o_ref):
    o_ref[0] = jnp.dot(u_ref[0].astype(BF16), s_ref[0], preferred_element_type=F32)


def _ssm_local(u2, smat):
    gp, nk, w = u2.shape
    n = smat.shape[2]
    return pl.pallas_call(
        _ssm_local_kernel,
        grid=(gp,),
        in_specs=[pl.BlockSpec((1, nk, w), lambda i: (i, 0, 0)),
                  pl.BlockSpec((1, w, n), lambda i: (i, 0, 0))],
        out_specs=pl.BlockSpec((1, nk, n), lambda i: (i, 0, 0)),
        out_shape=jax.ShapeDtypeStruct((gp, nk, n), F32),
        compiler_params=_cparams(("parallel",)),
        name="ssm_local",
    )(u2, smat)


def _ssm_rec_kernel(s_ref, are_ref, aim_ref, h_ref, fin_ref, *, nk):
    w = are_ref.shape[1]
    a_re = are_ref[...]
    a_im = aim_ref[...]

    def body(k, carry):
        hr, hi = carry
        h_ref[k] = jnp.concatenate([hr, hi], axis=1)
        sk = s_ref[k]
        return (a_re * hr - a_im * hi + sk[:, 0:w], a_re * hi + a_im * hr + sk[:, w:2 * w])

    z = jnp.zeros(a_re.shape, F32)
    hr, hi = lax.fori_loop(0, nk, body, (z, z))
    fin_ref[...] = jnp.concatenate([hr, hi], axis=1)


def _ssm_rec(s_kg, apow_re, apow_im):
    nk, gp, w2 = s_kg.shape
    return pl.pallas_call(
        functools.partial(_ssm_rec_kernel, nk=nk),
        out_shape=[jax.ShapeDtypeStruct((nk, gp, w2), F32),
                   jax.ShapeDtypeStruct((gp, w2), F32)],
        compiler_params=pltpu.CompilerParams(vmem_limit_bytes=VMEM_LIMIT),
        name="ssm_rec",
    )(s_kg, apow_re, apow_im)


def _ssm_out_kernel(u_ref, uh_ref, t_ref, h_ref, m_ref, d_ref, o_ref):
    y = jnp.dot(u_ref[0].astype(BF16), t_ref[0], preferred_element_type=F32)
    y = y + jnp.dot(h_ref[0].astype(BF16), m_ref[0], preferred_element_type=F32)
    o_ref[0] = y + d_ref[0] * uh_ref[0]


def _ssm_out(u2, tmat, hst, mmat, dtile):
    nb, nk, w = u2.shape
    hw = hst.shape[2]
    nh = 2
    wn = w // nh
    return pl.pallas_call(
        _ssm_out_kernel,
        grid=(nb, nh),
        in_specs=[pl.BlockSpec((1, nk, w), lambda i, j: (i, 0, 0)),
                  pl.BlockSpec((1, nk, wn), lambda i, j: (i, 0, j)),
                  pl.BlockSpec((1, w, wn), lambda i, j: (i, 0, j)),
                  pl.BlockSpec((1, nk, hw), lambda i, j: (i, 0, 0)),
                  pl.BlockSpec((1, hw, wn), lambda i, j: (i, 0, j)),
                  pl.BlockSpec((1, 1, wn), lambda i, j: (i, 0, j))],
        out_specs=pl.BlockSpec((1, nk, wn), lambda i, j: (i, 0, j)),
        out_shape=jax.ShapeDtypeStruct((nb, nk, w), F32),
        compiler_params=_cparams(("parallel", "arbitrary")),
        name="ssm_out",
    )(u2, u2, tmat, hst, mmat, dtile)


GBLK = LANES // SSM_GROUP


def _ssm_prompt(u, coefs, ssm_d):
    kall, s_re, s_im, cp_re, cp_im, _, _, apow_re, apow_im, _, _ = coefs
    t = u.shape[0]
    g, p, c, lc = N_SSM_GROUPS, SSM_STATE, SSM_GROUP, SCAN_CHUNK
    nk, nb = t // lc, g // GBLK
    hw = GBLK * p
    eye = jnp.eye(GBLK, dtype=F32)
    u2 = u.reshape(nk, lc, nb, LANES).transpose(2, 0, 1, 3).reshape(nb, nk, lc * LANES)

    def local_w(s):
        s7 = s.reshape(nb, GBLK, lc, c, 1, p).transpose(0, 2, 1, 3, 4, 5)
        return (s7 * eye[None, None, :, None, :, None]).reshape(nb, lc * LANES, hw)

    smat = jnp.concatenate([local_w(s_re), local_w(s_im)], axis=2).astype(BF16)
    s_loc = _ssm_local(u2, smat)
    hst, fin = _ssm_rec(s_loc.transpose(1, 0, 2), apow_re.reshape(nb, hw),
                        apow_im.reshape(nb, hw))
    k4 = kall.reshape(g, lc, c, c)
    idx = np.arange(lc)[None, :] - np.arange(lc)[:, None]
    tg = jnp.where((idx >= 0)[None, :, :, None, None], k4[:, np.maximum(idx, 0)], 0.0)
    tg = tg.reshape(nb, GBLK, lc, lc, c, c).transpose(0, 2, 1, 5, 3, 4)
    tmat = (tg[:, :, :, :, :, None, :] * eye[None, None, :, None, None, :, None])
    tmat = tmat.reshape(nb, lc * LANES, lc * LANES).astype(BF16)

    def read_w(cp):
        c6 = cp.reshape(nb, GBLK, lc, c, p).transpose(0, 1, 4, 2, 3)
        return (c6[:, :, :, :, None, :] * eye[None, :, None, None, :, None]).reshape(
            nb, hw, lc * LANES)

    mmat = jnp.concatenate([read_w(cp_re), read_w(-cp_im)], axis=1).astype(BF16)
    dtile = jnp.tile(ssm_d.reshape(nb, 1, LANES), (1, 1, lc))
    y2 = _ssm_out(u2, tmat, hst.transpose(1, 0, 2), mmat, dtile)
    y = y2.reshape(nb, nk, lc, LANES).transpose(1, 2, 0, 3).reshape(t, g * c)
    return y, fin[:, :hw].reshape(g, p), fin[:, hw:].reshape(g, p)


def _ssm_sample_kernel(u_ref, hr_ref, hi_ref, are_ref, aim_ref, bbr_ref, bbi_ref,
                       cr_ref, ci_ref, d_ref, y_ref, fr_ref, fi_ref, *, nb, steps):
    u = u_ref[0]
    nt = (((1,), (0,)), ((), ()))
    bu_re = lax.dot_general(u, bbr_ref[0], nt, precision=_HI, preferred_element_type=F32)
    bu_im = lax.dot_general(u, bbi_ref[0], nt, precision=_HI, preferred_element_type=F32)
    a_re, a_im = are_ref[0], aim_ref[0]
    hr, hi = hr_ref[0], hi_ref[0]
    hs_re, hs_im = [], []
    for l in range(steps):
        sl = slice(l * nb, (l + 1) * nb)
        hr, hi = (a_re * hr - a_im * hi + bu_re[sl], a_re * hi + a_im * hr + bu_im[sl])
        hs_re.append(hr)
        hs_im.append(hi)
    h_re = jnp.concatenate(hs_re, axis=0).astype(BF16)
    h_im = jnp.concatenate(hs_im, axis=0).astype(BF16)
    ntt = (((1,), (1,)), ((), ()))
    y = (lax.dot_general(h_re, cr_ref[0].astype(BF16), ntt, preferred_element_type=F32)
         - lax.dot_general(h_im, ci_ref[0].astype(BF16), ntt, preferred_element_type=F32))
    y_ref[0] = y + d_ref[0] * u
    fr_ref[0] = hr
    fi_ref[0] = hi


def _ssm_sample(u_g, h0_re, h0_im, coefs, c_re, c_im, ssm_d, nb, steps):
    abar_re, abar_im = coefs[5], coefs[6]
    bb_re, bb_im = coefs[9], coefs[10]
    g, p, c = N_SSM_GROUPS, SSM_STATE, SSM_GROUP
    rows = steps * nb
    sds = jax.ShapeDtypeStruct
    spec = lambda a, b: pl.BlockSpec((1, a, b), lambda i: (i, 0, 0))
    return pl.pallas_call(
        functools.partial(_ssm_sample_kernel, nb=nb, steps=steps),
        grid=(g,),
        in_specs=[spec(rows, c), spec(nb, p), spec(nb, p), spec(1, p), spec(1, p),
                  spec(c, p), spec(c, p), spec(c, p), spec(c, p), spec(1, c)],
        out_specs=[spec(rows, c), spec(nb, p), spec(nb, p)],
        out_shape=[sds((g, rows, c), F32), sds((g, nb, p), F32), sds((g, nb, p), F32)],
        compiler_params=_cparams(("parallel",)),
        name="ssm_sample",
    )(u_g, h0_re, h0_im, abar_re, abar_im, bb_re, bb_im, c_re, c_im,
      ssm_d.reshape(g, 1, c))


def _layer_norm(x, g, b):
    mu = jnp.mean(x, axis=-1, keepdims=True)
    var = jnp.mean(jnp.square(x - mu), axis=-1, keepdims=True)
    return (x - mu) * lax.rsqrt(var + LN_EPS) * g + b


def _merge_kernel(o_ref, y_ref, x_ref, gw_ref, gb_ref, wo_ref, g_ref, b_ref,
                  x1_ref, x1t_ref):
    z = jax.nn.gelu(y_ref[...])
    gate = jnp.dot(z.astype(BF16), gw_ref[...], preferred_element_type=F32) + gb_ref[...]
    z = z * jax.nn.sigmoid(gate)
    mix = jnp.dot(o_ref[...].astype(BF16), wo_ref[0:ATTN_WIDTH, :], preferred_element_type=F32)
    mix = mix + jnp.dot(z.astype(BF16), wo_ref[ATTN_WIDTH:, :], preferred_element_type=F32)
    x1 = _layer_norm(DEEPNORM_ALPHA * x_ref[...] + mix, g_ref[...], b_ref[...])
    x1_ref[...] = x1
    x1t_ref[...] = x1.T.astype(BF16)


def _merge(attn_o, y_ssm, x, glu_w, glu_b, w_out, ln_g, ln_b, tm):
    t = x.shape[0]
    full = lambda r, c: pl.BlockSpec((r, c), lambda i: (0, 0))
    return pl.pallas_call(
        _merge_kernel,
        grid=(t // tm,),
        in_specs=[pl.BlockSpec((tm, ATTN_WIDTH), lambda i: (i, 0)),
                  pl.BlockSpec((tm, SSM_WIDTH), lambda i: (i, 0)),
                  pl.BlockSpec((tm, D_MODEL), lambda i: (i, 0)),
                  full(SSM_WIDTH, SSM_WIDTH), full(1, SSM_WIDTH),
                  full(D_MODEL, D_MODEL), full(1, D_MODEL), full(1, D_MODEL)],
        out_specs=[pl.BlockSpec((tm, D_MODEL), lambda i: (i, 0)),
                   pl.BlockSpec((D_MODEL, tm), lambda i: (0, i))],
        out_shape=[jax.ShapeDtypeStruct((t, D_MODEL), F32),
                   jax.ShapeDtypeStruct((D_MODEL, t), BF16)],
        compiler_params=_cparams(("parallel",)),
        name="merge_out_ln1",
    )(attn_o, y_ssm, x, glu_w, glu_b, w_out, ln_g, ln_b)


def _top_desc(vals, n):
    rows = []
    for r in range(n):
        mx = jnp.max(vals, axis=0, keepdims=True)
        rows.append(mx)
        if r + 1 < n:
            vals = jnp.where(vals == mx, -jnp.inf, vals)
    return rows


def _route_kernel(xt_ref, wq_ref, keys_ref, s1_ref, s2_ref, e1_ref, e2_ref, thr_ref):
    qt = jnp.dot(wq_ref[...], xt_ref[...], preferred_element_type=F32).astype(BF16)
    thr_rows = []
    for h in range(PEER_HEADS):
        tops = []
        for m in range(2):
            hm = 2 * h + m
            sc = jnp.dot(keys_ref[hm], qt[hm * PEER_NKEYS:(hm + 1) * PEER_NKEYS, :],
                         preferred_element_type=F32)
            tops.append((sc, _top_desc(sc, PEER_TOPK)))
        (sc1, a), (sc2, b) = tops
        bmat = jnp.concatenate(b, axis=0)
        cand = jnp.concatenate([a[r] + bmat for r in range(PEER_TOPK)], axis=0)
        best = _top_desc(cand, PEER_TOPK)
        zsum = jnp.ones_like(best[0])
        for r in range(1, PEER_TOPK):
            zsum = zsum + jnp.exp(best[r] - best[0])
        s1_ref[h] = sc1
        s2_ref[h] = sc2
        e1_ref[h] = jnp.exp(sc1 - a[0]) / zsum
        e2_ref[h] = jnp.exp(sc2 - b[0])
        thr_rows.append(best[PEER_TOPK - 1])
    thr_ref[...] = jnp.concatenate(thr_rows, axis=0)


def _route(x1t, wq_t, keys, tm):
    t = x1t.shape[1]
    big = pl.BlockSpec((PEER_HEADS, PEER_NKEYS, tm), lambda i: (0, 0, i))
    sds = jax.ShapeDtypeStruct((PEER_HEADS, PEER_NKEYS, t), F32)
    return pl.pallas_call(
        _route_kernel,
        grid=(t // tm,),
        in_specs=[pl.BlockSpec((D_MODEL, tm), lambda i: (0, i)),
                  pl.BlockSpec(wq_t.shape, lambda i: (0, 0)),
                  pl.BlockSpec(keys.shape, lambda i: (0, 0, 0))],
        out_specs=[big, big, big, big, pl.BlockSpec((PEER_HEADS, tm), lambda i: (0, i))],
        out_shape=[sds, sds, sds, sds, jax.ShapeDtypeStruct((PEER_HEADS, t), F32)],
        compiler_params=_cparams(("parallel",)),
        name="peer_route",
    )(x1t, wq_t, keys)


IPB = 4


def _peer_kernel(xt_ref, u_ref, vt_ref, s1_ref, s2_ref, e1_ref, e2_ref, thr_ref,
                 o_ref, *, tsub):
    eb = pl.program_id(1)

    @pl.when(eb == 0)
    def _zero():
        o_ref[...] = jnp.zeros(o_ref.shape, F32)

    for j in range(xt_ref.shape[1] // tsub):
        cols = slice(j * tsub, (j + 1) * tsub)
        a_rows = []
        for ii in range(IPB):
            i = eb * IPB + ii
            rows = slice(0, PEER_NKEYS)
            ht = jnp.dot(u_ref[ii * PEER_NKEYS:(ii + 1) * PEER_NKEYS, :], xt_ref[:, cols],
                         preferred_element_type=F32)
            a_cols = []
            s1rows = [s1_ref[h, pl.ds(i, 1), cols] for h in range(PEER_HEADS)]
            e1rows = [e1_ref[h, pl.ds(i, 1), cols] for h in range(PEER_HEADS)]
            for c0 in range(0, tsub, LANES):
                hl = slice(c0, c0 + LANES)
                lc = slice(j * tsub + c0, j * tsub + c0 + LANES)
                w = jnp.zeros((PEER_NKEYS, LANES), F32)
                for h in range(PEER_HEADS):
                    pair = s2_ref[h, :, lc] + s1rows[h][:, hl]
                    w = w + jnp.where(pair >= thr_ref[h:h + 1, lc],
                                      e2_ref[h, :, lc] * e1rows[h][:, hl], 0.0)
                a_cols.append((w * jax.nn.gelu(ht[rows, hl])).astype(BF16))
            a_rows.append(a_cols[0] if len(a_cols) == 1 else jnp.concatenate(a_cols, axis=1))
        a = jnp.concatenate(a_rows, axis=0)
        o_ref[:, cols] += jnp.dot(vt_ref[...], a, preferred_element_type=F32)


def _peer(x1t, u_tab, vt_tab, s1, s2, e1, e2, thr, tm, tsub):
    t = x1t.shape[1]
    te = IPB * PEER_NKEYS
    n_eb = u_tab.shape[0] // te
    once = pl.Buffered(1)
    big = pl.BlockSpec((PEER_HEADS, PEER_NKEYS, tm), lambda i, e: (0, 0, i), pipeline_mode=once)
    return pl.pallas_call(
        functools.partial(_peer_kernel, tsub=tsub),
        grid=(t // tm, n_eb),
        in_specs=[pl.BlockSpec((D_MODEL, tm), lambda i, e: (0, i), pipeline_mode=once),
                  pl.BlockSpec((te, D_MODEL), lambda i, e: (e, 0)),
                  pl.BlockSpec((D_MODEL, te), lambda i, e: (0, e)),
                  big, big, big, big,
                  pl.BlockSpec((PEER_HEADS, tm), lambda i, e: (0, i))],
        out_specs=pl.BlockSpec((D_MODEL, tm), lambda i, e: (0, i)),
        out_shape=jax.ShapeDtypeStruct((D_MODEL, t), F32),
        compiler_params=_cparams(("parallel", "arbitrary")),
        name="peer_experts",
    )(x1t, u_tab, vt_tab, s1, s2, e1, e2, thr)


def _ln2_kernel(x1_ref, ft_ref, g_ref, b_ref, o_ref):
    o_ref[...] = _layer_norm(DEEPNORM_ALPHA * x1_ref[...] + ft_ref[...].T,
                             g_ref[...], b_ref[...])


def _ln2(x1, ffn_t, ln_g, ln_b, tm):
    t = x1.shape[0]
    return pl.pallas_call(
        _ln2_kernel,
        grid=(t // tm,),
        in_specs=[pl.BlockSpec((tm, D_MODEL), lambda i: (i, 0)),
                  pl.BlockSpec((D_MODEL, tm), lambda i: (0, i)),
                  pl.BlockSpec((1, D_MODEL), lambda i: (0, 0)),
                  pl.BlockSpec((1, D_MODEL), lambda i: (0, 0))],
        out_specs=pl.BlockSpec((tm, D_MODEL), lambda i: (i, 0)),
        out_shape=jax.ShapeDtypeStruct((t, D_MODEL), F32),
        compiler_params=_cparams(("parallel",)),
        name="ln2",
    )(x1, ffn_t, ln_g, ln_b)


def _prompt_buckets(tb):
    r = np.arange(tb)[:, None]
    c = np.arange(tb)[None, :]
    diag = np.where(c <= r, _bucket_np(r - c), -1)
    sub = _bucket_np(r - c + tb)
    return np.stack([diag, sub]).astype(np.int32)


def _sample_buckets(past_len, dec_seq):
    qpos = past_len + np.tile(np.arange(dec_seq), 2)[:, None]
    w = PPS * PAGE_SIZE
    kpos = (past_len - w) + np.arange(w)[None, :]
    last = _bucket_np(qpos - kpos)
    jn = np.arange(NEWK)[None, :]
    knew = past_len + jn
    new = np.where((jn < dec_seq) & (knew <= qpos), _bucket_np(qpos - knew), -1)
    return last[None].astype(np.int32), new[None].astype(np.int32)


def _group_tail(x1, x1t, w_query_t, keys, u_tab, vt_tab, ln_g, ln_b, tm_route, tm_peer, tm_ln):
    s1, s2, e1, e2, thr = _route(x1t, w_query_t, keys, tm_route)
    ffn_t = _peer(x1t, u_tab, vt_tab, s1, s2, e1, e2, thr, tm_peer, min(tm_peer, 256))
    return _ln2(x1, ffn_t, ln_g, ln_b, tm_ln)


def kernel(x_prompt, x_sample, cache_k, cache_v, state_ssm_re, state_ssm_im, page_table, w_in, lambda_q1, lambda_k1, lambda_q2, lambda_k2, attn_subln_g, rel_bias, ssm_a_re, ssm_a_im, ssm_b_re, ssm_b_im, ssm_c_re, ssm_c_im, ssm_d, ssm_log_dt, ssm_glu_w, ssm_glu_b, w_out, ln1_g, ln1_b, peer_w_query, peer_sub_keys, peer_u, peer_v, ln2_g, ln2_b):
    l = 0
    seq = x_prompt.shape[1]
    dec_b, dec_s = x_sample.shape[0], x_sample.shape[1]
    n_pages = page_table.shape[1]
    past_len = n_pages * PAGE_SIZE
    g, p, c = N_SSM_GROUPS, SSM_STATE, SSM_GROUP
    lam_init = 0.8 - 0.6 * math.exp(-0.3 * l)
    lam = (jnp.exp(jnp.sum(lambda_q1[l] * lambda_k1[l]))
           - jnp.exp(jnp.sum(lambda_q2[l] * lambda_k2[l])) + lam_init).reshape(1).astype(F32)

    w_in_b = w_in[l].astype(BF16)
    glu_w_b = ssm_glu_w[l].astype(BF16)
    w_out_b = w_out[l].astype(BF16)
    wq_t = peer_w_query[l].T.astype(BF16)
    keys = peer_sub_keys[l].reshape(2 * PEER_HEADS, PEER_NKEYS, PEER_NKEYS).astype(BF16)
    u_tab = peer_u[l].astype(BF16)
    vt_tab = peer_v[l].T.astype(BF16)
    subln_g = attn_subln_g[l].reshape(1, HEAD_DIM)
    glu_b = ssm_glu_b[l].reshape(1, SSM_WIDTH)
    g1, b1 = ln1_g[l].reshape(1, D_MODEL), ln1_b[l].reshape(1, D_MODEL)
    g2, b2 = ln2_g[l].reshape(1, D_MODEL), ln2_b[l].reshape(1, D_MODEL)
    coefs = _ssm_coefs(ssm_log_dt[l], ssm_a_re[l], ssm_a_im[l],
                       ssm_b_re[l].transpose(0, 2, 1), ssm_b_im[l].transpose(0, 2, 1),
                       ssm_c_re[l], ssm_c_im[l])

    xp = x_prompt.reshape(seq, D_MODEL)
    tb = 512
    qkvu = _proj(xp, w_in_b, 512, ATTN_WIDTH)
    bias_p = _bias_tiles(rel_bias, _prompt_buckets(tb))
    attn_p = _attn_prompt(qkvu, bias_p, lam, subln_g, tb, lam_init)
    y_p, hr_p, hi_p = _ssm_prompt(qkvu[3], coefs, ssm_d[l])
    x1_p, x1t_p = _merge(attn_p, y_p, xp, glu_w_b, glu_b, w_out_b, g1, b1, 256)
    out_p = _group_tail(x1_p, x1t_p, wq_t, keys, u_tab, vt_tab, g2, b2, 256, 1024, 256)

    ts = dec_b * dec_s
    xs = x_sample.reshape(ts, D_MODEL)
    qkvu_s = _proj(xs, w_in_b, ts, ATTN_WIDTH)
    q_s = qkvu_s[0].reshape(dec_b, dec_s, N_HEADS, HEAD_DIM) * (HALF_DIM ** -0.5)
    lane = np.arange(HEAD_DIM)
    halves = jnp.asarray(np.stack([lane < HALF_DIM, lane >= HALF_DIM]).astype(np.float32))
    q2 = (q_s[:, None] * halves[None, :, None, None, :])
    q2 = q2.transpose(0, 3, 1, 2, 4).reshape(dec_b, N_HEADS, QROWS, HEAD_DIM)
    pad_new = lambda a: jnp.pad(
        a.reshape(dec_b, dec_s, N_HEADS, HEAD_DIM).transpose(0, 2, 1, 3),
        ((0, 0), (0, 0), (0, NEWK - dec_s), (0, 0)))
    bl, bn = _sample_buckets(past_len, dec_s)
    bias_last = _bias_tiles(rel_bias, bl)[:, 0]
    bias_new = _bias_tiles(rel_bias, bn)[:, 0]
    attn_s = _attn_sample(q2, pad_new(qkvu_s[1]), pad_new(qkvu_s[2]), cache_k[l], cache_v[l],
                          page_table, bias_last, bias_new, lam, subln_g, lam_init)
    u_s = qkvu_s[3].reshape(dec_b, dec_s, g, c).transpose(2, 1, 0, 3).reshape(g, ts, c)
    y_g, hr_s, hi_s = _ssm_sample(u_s, state_ssm_re[l].transpose(1, 0, 2),
                                  state_ssm_im[l].transpose(1, 0, 2), coefs,
                                  ssm_c_re[l], ssm_c_im[l], ssm_d[l], dec_b, dec_s)
    y_s = y_g.reshape(g, dec_s, dec_b, c).transpose(2, 1, 0, 3).reshape(ts, SSM_WIDTH)
    x1_s, x1t_s = _merge(attn_s.reshape(ts, ATTN_WIDTH), y_s, xs, glu_w_b, glu_b, w_out_b,
                         g1, b1, ts)
    out_s = _group_tail(x1_s, x1t_s, wq_t, keys, u_tab, vt_tab, g2, b2, ts, ts, ts)

    kv = lambda a, b_, s_: a.reshape(1, b_, s_, N_HEADS, HEAD_DIM)
    return (out_p.reshape(1, seq, D_MODEL), out_s.reshape(dec_b, dec_s, D_MODEL),
            kv(qkvu[1], 1, seq), kv(qkvu[2], 1, seq),
            hr_p.reshape(1, 1, g, p), hi_p.reshape(1, 1, g, p),
            kv(qkvu_s[1], dec_b, dec_s), kv(qkvu_s[2], dec_b, dec_s),
            hr_s.transpose(1, 0, 2)[None], hi_s.transpose(1, 0, 2)[None])
```

```python
import functools
import math

import numpy as np
import jax
import jax.numpy as jnp
from jax import lax
from jax.experimental import pallas as pl
from jax.experimental.pallas import tpu as pltpu

F32 = jnp.float32
BF16 = jnp.bfloat16

D_MODEL = 2048
PAGE_SIZE = 128
ATTN_WIDTH = 1024
SSM_WIDTH = 1024
N_HEADS = 8
HEAD_DIM = 128
HALF_DIM = 64
N_BUCKETS = 32
MAX_DISTANCE = 128
SSM_GROUP = 16
N_SSM_GROUPS = 64
SSM_STATE = 64
PEER_HEADS = 8
PEER_NKEYS = 128
PEER_TOPK = 16
LN_EPS = 1e-5
DEPTH = 1
DEEPNORM_ALPHA = (2.0 * DEPTH) ** 0.25

LANES = 128
SCAN_CHUNK = 16
GBLK = LANES // SSM_GROUP
NEG = -1e30
VMEM_LIMIT = 56 * 1024 * 1024

_HI = lax.Precision.HIGHEST


def _cparams(sem):
    return pltpu.CompilerParams(dimension_semantics=sem, vmem_limit_bytes=VMEM_LIMIT)


def _rep(x, n):
    return x if n == 1 else jnp.concatenate([x] * n, axis=1)


def _proj_kernel(x_ref, w_ref, o_ref):
    o_ref[0] = jnp.dot(x_ref[...].astype(BF16), w_ref[...], preferred_element_type=F32)


def _proj(x, w, tm, tn):
    t, k = x.shape
    n = w.shape[1]
    return pl.pallas_call(
        _proj_kernel,
        grid=(t // tm, n // tn),
        in_specs=[pl.BlockSpec((tm, k), lambda i, j: (i, 0)),
                  pl.BlockSpec((k, tn), lambda i, j: (0, j))],
        out_specs=pl.BlockSpec((1, tm, tn), lambda i, j: (j, i, 0)),
        out_shape=jax.ShapeDtypeStruct((n // tn, t, tn), F32),
        compiler_params=_cparams(("parallel", "arbitrary")),
        name="in_proj",
    )(x, w)


def _bucket_np(d):
    n = np.maximum(d, 0)
    max_exact = N_BUCKETS // 2
    nf = np.maximum(n, 1).astype(np.float64)
    large = max_exact + (np.log(nf / max_exact) / math.log(MAX_DISTANCE / max_exact)
                         * (N_BUCKETS - max_exact)).astype(np.int32)
    large = np.minimum(large, N_BUCKETS - 1)
    return np.where(n < max_exact, n, large).astype(np.int32)


def _bias_kernel(rb_ref, bkt_ref, o_ref):
    h = pl.program_id(0)
    bkt = bkt_ref[0]
    far = rb_ref[N_BUCKETS - 1, h]
    acc = jnp.where(bkt < 0, NEG, 0.0).astype(F32)
    for b in range(N_BUCKETS - 1):
        acc = jnp.where(bkt == b, rb_ref[b, h] - far, acc)
    o_ref[0, 0] = acc


def _bias_tiles(rel_bias, bkt):
    n, r, c = bkt.shape
    return pl.pallas_call(
        _bias_kernel,
        grid=(N_HEADS, n),
        in_specs=[pl.BlockSpec(memory_space=pltpu.SMEM),
                  pl.BlockSpec((1, r, c), lambda h, i: (i, 0, 0))],
        out_specs=pl.BlockSpec((1, 1, r, c), lambda h, i: (h, i, 0, 0)),
        out_shape=jax.ShapeDtypeStruct((N_HEADS, n, r, c), F32),
        compiler_params=_cparams(("parallel", "arbitrary")),
        name="bias_tiles",
    )(rel_bias, jnp.asarray(bkt))


def _subln(attn, g_row, lam_init):
    ms = jnp.mean(jnp.square(attn), axis=-1, keepdims=True)
    return attn * lax.rsqrt(ms + LN_EPS) * g_row * (1.0 - lam_init)


def _attn_kernel(qi_ref, ki_ref, lam_ref, q_ref, k_ref, v_ref, bias_ref, g_ref, o_ref,
                 q2_sc, m_sc, l_sc, acc_sc, *, tb, lam_init):
    s = pl.program_id(1)
    qi = qi_ref[s]
    ki = ki_ref[s]

    @pl.when(ki == 0)
    def _init():
        q = q_ref[0] * (HALF_DIM ** -0.5)
        lane = lax.broadcasted_iota(jnp.int32, q.shape, 1)
        q2_sc[0:tb, :] = jnp.where(lane < HALF_DIM, q, 0.0).astype(BF16)
        q2_sc[tb:2 * tb, :] = jnp.where(lane >= HALF_DIM, q, 0.0).astype(BF16)
        m_sc[...] = jnp.full(m_sc.shape, NEG, F32)
        l_sc[...] = jnp.zeros(l_sc.shape, F32)
        acc_sc[...] = jnp.zeros(acc_sc.shape, F32)

    def logits():
        kb = k_ref[0].astype(BF16)
        return lax.dot_general(q2_sc[...], kb, (((1,), (1,)), ((), ())),
                               preferred_element_type=F32)

    def update(sc):
        m_prev = m_sc[...]
        m_new = jnp.maximum(m_prev, jnp.max(sc, axis=-1, keepdims=True))
        alpha = jnp.exp(m_prev - m_new)
        p = jnp.exp(sc - _rep(m_new, tb // LANES))
        l_sc[...] = alpha * l_sc[...] + jnp.sum(p, axis=-1, keepdims=True)
        acc_sc[...] = alpha * acc_sc[...] + jnp.dot(
            p.astype(BF16), v_ref[0].astype(BF16), preferred_element_type=F32)
        m_sc[...] = m_new

    @pl.when(ki < qi - 1)
    def _far():
        update(logits())

    @pl.when(ki == qi - 1)
    def _sub():
        b = bias_ref[0, 1]
        update(logits() + jnp.concatenate([b, b], axis=0))

    @pl.when(ki == qi)
    def _diag():
        b = bias_ref[0, 0]
        update(logits() + jnp.concatenate([b, b], axis=0))
        o = acc_sc[...] / l_sc[...]
        attn = o[0:tb] - lam_ref[0] * o[tb:2 * tb]
        o_ref[...] = _subln(attn, g_ref[...], lam_init)


def _attn_prompt(qkvu, bias, lam, subln_g, tb, lam_init):
    t = qkvu.shape[1]
    nb = t // tb
    qi_idx = np.concatenate([np.full(i + 1, i) for i in range(nb)]).astype(np.int32)
    ki_idx = np.concatenate([np.arange(i + 1) for i in range(nb)]).astype(np.int32)
    grid_spec = pltpu.PrefetchScalarGridSpec(
        num_scalar_prefetch=2,
        grid=(N_HEADS, len(qi_idx)),
        in_specs=[
            pl.BlockSpec(memory_space=pltpu.SMEM),
            pl.BlockSpec((1, tb, HEAD_DIM), lambda h, s, qi, ki: (0, qi[s], h)),
            pl.BlockSpec((1, tb, HEAD_DIM), lambda h, s, qi, ki: (1, ki[s], h)),
            pl.BlockSpec((1, tb, HEAD_DIM), lambda h, s, qi, ki: (2, ki[s], h)),
            pl.BlockSpec((1, 2, tb, tb), lambda h, s, qi, ki: (h, 0, 0, 0)),
            pl.BlockSpec((1, HEAD_DIM), lambda h, s, qi, ki: (0, 0)),
        ],
        out_specs=pl.BlockSpec((tb, HEAD_DIM), lambda h, s, qi, ki: (qi[s], h)),
        scratch_shapes=[
            pltpu.VMEM((2 * tb, HEAD_DIM), BF16),
            pltpu.VMEM((2 * tb, LANES), F32),
            pltpu.VMEM((2 * tb, LANES), F32),
            pltpu.VMEM((2 * tb, HEAD_DIM), F32),
        ],
    )
    return pl.pallas_call(
        functools.partial(_attn_kernel, tb=tb, lam_init=lam_init),
        grid_spec=grid_spec,
        out_shape=jax.ShapeDtypeStruct((t, ATTN_WIDTH), F32),
        compiler_params=_cparams(("parallel", "arbitrary")),
        name="attn_prompt",
    )(jnp.asarray(qi_idx), jnp.asarray(ki_idx), lam, qkvu, qkvu, qkvu, bias, subln_g)


PPS = 8
QROWS = 8
NEWK = 16


def _decode_kernel(pt_ref, lam_ref, q2_ref, kn_ref, vn_ref, bl_ref, bn_ref, g_ref, *rest,
                   n_steps, lam_init):
    k_refs = rest[0:PPS]
    v_refs = rest[PPS:2 * PPS]
    o_ref = rest[2 * PPS]
    m_sc, l_sc, acc_sc = rest[2 * PPS + 1:]
    p_idx = pl.program_id(1)
    last = p_idx == n_steps - 1

    @pl.when(p_idx == 0)
    def _init():
        m_sc[...] = jnp.full(m_sc.shape, NEG, F32)
        l_sc[...] = jnp.zeros(l_sc.shape, F32)
        acc_sc[...] = jnp.zeros(acc_sc.shape, F32)

    last_f = jnp.where(last, 1.0, 0.0).astype(F32)

    def softmax_step(h, sc):
        m_prev = m_sc[h]
        m_new = jnp.maximum(m_prev, jnp.max(sc, axis=-1, keepdims=True))
        alpha = jnp.exp(m_prev - m_new)
        p = jnp.exp(sc - _rep(m_new, sc.shape[1] // LANES) if sc.shape[1] >= LANES
                    else sc - m_new[:, 0:sc.shape[1]])
        l_sc[h] = alpha * l_sc[h] + jnp.sum(p, axis=-1, keepdims=True)
        m_sc[h] = m_new
        return p.astype(BF16), alpha

    def accumulate(h, p, alpha, vmat):
        acc_sc[h] = alpha * acc_sc[h] + jnp.dot(p, vmat, preferred_element_type=F32)

    def update(h, sc, vmat):
        p, alpha = softmax_step(h, sc)
        accumulate(h, p, alpha, vmat)

    def head_rows(refs, h):
        rows_h = pl.ds(h, PAGE_SIZE, stride=N_HEADS)
        return jnp.concatenate([r[0, rows_h, :] for r in refs], axis=0).astype(BF16)

    scores = []
    for h in range(N_HEADS):
        q2 = q2_ref[0, h].astype(BF16)
        sc = lax.dot_general(q2, head_rows(k_refs, h), (((1,), (1,)), ((), ())),
                             preferred_element_type=F32)
        scores.append(sc + bl_ref[h] * last_f)
    probs = [softmax_step(h, scores[h]) for h in range(N_HEADS)]
    for h in range(N_HEADS):
        accumulate(h, probs[h][0], probs[h][1], head_rows(v_refs, h))

    @pl.when(last)
    def _finish():
        for h in range(N_HEADS):
            q2 = q2_ref[0, h].astype(BF16)
            kn = kn_ref[0, h].astype(BF16)
            vn = vn_ref[0, h].astype(BF16)
            sc = lax.dot_general(q2, kn, (((1,), (1,)), ((), ())), preferred_element_type=F32)
            update(h, sc + bn_ref[h], vn)
            o = acc_sc[h] / l_sc[h]
            half = QROWS // 2
            attn = o[0:half] - lam_ref[0] * o[half:QROWS]
            o_ref[0, :, h * HEAD_DIM:(h + 1) * HEAD_DIM] = _subln(attn, g_ref[...], lam_init)


def _attn_sample(q2, k_new, v_new, cache_k, cache_v, page_table, bias_last, bias_new,
                 lam, subln_g, lam_init):
    nb, n_pages = page_table.shape
    n_steps = n_pages // PPS
    dec_seq = QROWS // 2
    pages = lambda cache: cache.reshape(cache.shape[0], PAGE_SIZE * N_HEADS, HEAD_DIM)

    def page_spec(j):
        return pl.BlockSpec(
            (1, PAGE_SIZE * N_HEADS, HEAD_DIM),
            lambda b, p, pt, j=j: (pt[b * n_pages + p * PPS + j], 0, 0))

    grid_spec = pltpu.PrefetchScalarGridSpec(
        num_scalar_prefetch=1,
        grid=(nb, n_steps),
        in_specs=[
            pl.BlockSpec(memory_space=pltpu.SMEM),
            pl.BlockSpec((1, N_HEADS, QROWS, HEAD_DIM), lambda b, p, pt: (b, 0, 0, 0)),
            pl.BlockSpec((1, N_HEADS, NEWK, HEAD_DIM), lambda b, p, pt: (b, 0, 0, 0)),
            pl.BlockSpec((1, N_HEADS, NEWK, HEAD_DIM), lambda b, p, pt: (b, 0, 0, 0)),
            pl.BlockSpec((N_HEADS, QROWS, PPS * PAGE_SIZE), lambda b, p, pt: (0, 0, 0)),
            pl.BlockSpec((N_HEADS, QROWS, NEWK), lambda b, p, pt: (0, 0, 0)),
            pl.BlockSpec((1, HEAD_DIM), lambda b, p, pt: (0, 0)),
        ] + [page_spec(j) for j in range(PPS)] + [page_spec(j) for j in range(PPS)],
        out_specs=pl.BlockSpec((1, dec_seq, ATTN_WIDTH), lambda b, p, pt: (b, 0, 0)),
        scratch_shapes=[
            pltpu.VMEM((N_HEADS, QROWS, LANES), F32),
            pltpu.VMEM((N_HEADS, QROWS, LANES), F32),
            pltpu.VMEM((N_HEADS, QROWS, HEAD_DIM), F32),
        ],
    )
    return pl.pallas_call(
        functools.partial(_decode_kernel, n_steps=n_steps, lam_init=lam_init),
        grid_spec=grid_spec,
        out_shape=jax.ShapeDtypeStruct((nb, dec_seq, ATTN_WIDTH), F32),
        compiler_params=_cparams(("parallel", "arbitrary")),
        name="attn_sample",
    )(page_table.reshape(-1), lam, q2, k_new, v_new, bias_last, bias_new, subln_g,
      *([pages(cache_k)] * PPS), *([pages(cache_v)] * PPS))


def _ssm_coef_kernel(ldt_ref, ar_ref, ai_ref, bt_re_ref, bt_im_ref, c_re_ref, c_im_ref,
                     t_ref, s_ref, m_ref,
                     abar_re_ref, abar_im_ref, apow_re_ref, apow_im_ref,
                     bb_re_ref, bb_im_ref):
    lc, c, p = SCAN_CHUNK, SSM_GROUP, SSM_STATE
    hw = GBLK * p
    dt = jnp.exp(ldt_ref[0])
    ar, ai = ar_ref[0], ai_ref[0]
    mag = jnp.exp(ar * dt)
    abar_re = mag * jnp.cos(ai * dt)
    abar_im = mag * jnp.sin(ai * dt)
    nr, ni = abar_re - 1.0, abar_im
    den = ar * ar + ai * ai
    f_re = ((nr * ar + ni * ai) / den)[:, None, :]
    f_im = ((ni * ar - nr * ai) / den)[:, None, :]
    bt_re, bt_im = bt_re_ref[0], bt_im_ref[0]
    bb_re = f_re * bt_re - f_im * bt_im
    bb_im = f_re * bt_im + f_im * bt_re
    c_re, c_im = c_re_ref[0], c_im_ref[0]

    pw = [(jnp.ones_like(abar_re), jnp.zeros_like(abar_re))]
    for _ in range(lc):
        pr, pi = pw[-1]
        pw.append((pr * abar_re - pi * abar_im, pr * abar_im + pi * abar_re))

    rid = lax.broadcasted_iota(jnp.int32, (LANES, hw), 0) // c
    cid = lax.broadcasted_iota(jnp.int32, (LANES, hw), 1) // p
    same_group = rid == cid

    def bdiag(x):
        flat = x.reshape(LANES, p)
        return jnp.where(same_group, jnp.concatenate([flat] * GBLK, axis=1), 0.0)

    def cmul(xr, xi, w):
        wr, wi = w[0][:, None, :], w[1][:, None, :]
        return xr * wr - xi * wi, xr * wi + xi * wr

    l_re, l_im = bdiag(bb_re), bdiag(bb_im)
    nt = (((1,), (1,)), ((), ()))
    zero_tile = jnp.zeros((LANES, LANES), BF16)
    bd = []
    for tau in range(lc + 1):
        cr, ci = cmul(c_re, c_im, pw[tau])
        r_re, r_im = bdiag(cr), bdiag(ci)
        if tau < lc:
            bd.append((lax.dot_general(l_re, r_re, nt, precision=_HI, preferred_element_type=F32)
                       - lax.dot_general(l_im, r_im, nt, precision=_HI,
                                         preferred_element_type=F32)).astype(BF16))
        if tau > 0:
            cols = slice((tau - 1) * LANES, tau * LANES)
            m_ref[0, 0:hw, cols] = r_re.T.astype(BF16)
            m_ref[0, hw:2 * hw, cols] = (-r_im).T.astype(BF16)
    for s in range(lc):
        rows = slice(s * LANES, (s + 1) * LANES)
        for r in range(lc):
            t_ref[0, rows, r * LANES:(r + 1) * LANES] = bd[r - s] if r >= s else zero_tile
        wr, wi = cmul(bb_re, bb_im, pw[lc - 1 - s])
        s_ref[0, rows, 0:hw] = bdiag(wr).astype(BF16)
        s_ref[0, rows, hw:2 * hw] = bdiag(wi).astype(BF16)
    abar_re_ref[0] = abar_re
    abar_im_ref[0] = abar_im
    apow_re_ref[0] = pw[lc][0]
    apow_im_ref[0] = pw[lc][1]
    bb_re_ref[0] = bb_re
    bb_im_ref[0] = bb_im


def _ssm_coefs(log_dt, a_re, a_im, bt_re, bt_im, c_re, c_im):
    g, p, c, lc = N_SSM_GROUPS, SSM_STATE, SSM_GROUP, SCAN_CHUNK
    nb, hw, w = g // GBLK, GBLK * p, lc * LANES
    vec = pl.BlockSpec((1, GBLK, p), lambda i: (i, 0, 0))
    mat = pl.BlockSpec((1, GBLK, c, p), lambda i: (i, 0, 0, 0))
    sds = jax.ShapeDtypeStruct
    blk = lambda a: a.reshape((nb, GBLK) + a.shape[1:])
    return pl.pallas_call(
        _ssm_coef_kernel,
        grid=(nb,),
        in_specs=[pl.BlockSpec((1, GBLK, 1), lambda i: (i, 0, 0)), vec, vec, mat, mat, mat, mat],
        out_specs=[pl.BlockSpec((1, w, w), lambda i: (i, 0, 0)),
                   pl.BlockSpec((1, w, 2 * hw), lambda i: (i, 0, 0)),
                   pl.BlockSpec((1, 2 * hw, w), lambda i: (i, 0, 0)),
                   vec, vec, vec, vec, mat, mat],
        out_shape=[sds((nb, w, w), BF16), sds((nb, w, 2 * hw), BF16), sds((nb, 2 * hw, w), BF16)]
        + [sds((nb, GBLK, p), F32)] * 4 + [sds((nb, GBLK, c, p), F32)] * 2,
        compiler_params=_cparams(("parallel",)),
        name="ssm_coefs",
    )(log_dt.reshape(nb, GBLK, 1), blk(a_re), blk(a_im), blk(bt_re), blk(bt_im),
      blk(c_re), blk(c_im))


def _ssm_local_kernel(u_ref, s_ref, o_ref):
    o_ref[0] = jnp.dot(u_ref[0].astype(BF16), s_ref[0], preferred_element_type=F32)


def _ssm_local(u2, smat):
    gp, nk, w = u2.shape
    n = smat.shape[2]
    return pl.pallas_call(
        _ssm_local_kernel,
        grid=(gp,),
        in_specs=[pl.BlockSpec((1, nk, w), lambda i: (i, 0, 0)),
                  pl.BlockSpec((1, w, n), lambda i: (i, 0, 0))],
        out_specs=pl.BlockSpec((1, nk, n), lambda i: (i, 0, 0)),
        out_shape=jax.ShapeDtypeStruct((gp, nk, n), F32),
        compiler_params=_cparams(("parallel",)),
        name="ssm_local",
    )(u2, smat)


def _ssm_rec_kernel(s_ref, are_ref, aim_ref, h_ref, fin_ref, *, nk):
    w = are_ref.shape[1]
    a_re = are_ref[...]
    a_im = aim_ref[...]

    def body(k, carry):
        hr, hi = carry
        h_ref[k] = jnp.concatenate([hr, hi], axis=1)
        sk = s_ref[k]
        return (a_re * hr - a_im * hi + sk[:, 0:w], a_re * hi + a_im * hr + sk[:, w:2 * w])

    z = jnp.zeros(a_re.shape, F32)
    hr, hi = lax.fori_loop(0, nk, body, (z, z))
    fin_ref[...] = jnp.concatenate([hr, hi], axis=1)


def _ssm_rec(s_kg, apow_re, apow_im):
    nk, gp, w2 = s_kg.shape
    return pl.pallas_call(
        functools.partial(_ssm_rec_kernel, nk=nk),
        out_shape=[jax.ShapeDtypeStruct((nk, gp, w2), F32),
                   jax.ShapeDtypeStruct((gp, w2), F32)],
        compiler_params=pltpu.CompilerParams(vmem_limit_bytes=VMEM_LIMIT),
        name="ssm_rec",
    )(s_kg, apow_re, apow_im)


def _ssm_out_kernel(u_ref, uh_ref, t_ref, h_ref, m_ref, d_ref, o_ref):
    y = jnp.dot(u_ref[0].astype(BF16), t_ref[0], preferred_element_type=F32)
    y = y + jnp.dot(h_ref[0].astype(BF16), m_ref[0], preferred_element_type=F32)
    o_ref[0] = y + d_ref[0] * uh_ref[0]


def _ssm_out(u2, tmat, hst, mmat, dtile):
    nb, nk, w = u2.shape
    hw = hst.shape[2]
    nh = 2
    wn = w // nh
    return pl.pallas_call(
        _ssm_out_kernel,
        grid=(nb, nh),
        in_specs=[pl.BlockSpec((1, nk, w), lambda i, j: (i, 0, 0)),
                  pl.BlockSpec((1, nk, wn), lambda i, j: (i, 0, j)),
                  pl.BlockSpec((1, w, wn), lambda i, j: (i, 0, j)),
                  pl.BlockSpec((1, nk, hw), lambda i, j: (i, 0, 0)),
                  pl.BlockSpec((1, hw, wn), lambda i, j: (i, 0, j)),
                  pl.BlockSpec((1, 1, wn), lambda i, j: (i, 0, j))],
        out_specs=pl.BlockSpec((1, nk, wn), lambda i, j: (i, 0, j)),
        out_shape=jax.ShapeDtypeStruct((nb, nk, w), F32),
        compiler_params=_cparams(("parallel", "arbitrary")),
        name="ssm_out",
    )(u2, u2, tmat, hst, mmat, dtile)


def _ssm_prompt(u, coefs, ssm_d):
    tmat, smat, mmat, _, _, apow_re, apow_im, _, _ = coefs
    t = u.shape[0]
    g, p, lc = N_SSM_GROUPS, SSM_STATE, SCAN_CHUNK
    nk, nb = t // lc, g // GBLK
    hw = GBLK * p
    u2 = u.reshape(nk, lc, nb, LANES).transpose(2, 0, 1, 3).reshape(nb, nk, lc * LANES)
    s_loc = _ssm_local(u2, smat)
    hst, fin = _ssm_rec(s_loc.transpose(1, 0, 2), apow_re.reshape(nb, hw),
                        apow_im.reshape(nb, hw))
    dtile = jnp.tile(ssm_d.reshape(nb, 1, LANES), (1, 1, lc))
    y2 = _ssm_out(u2, tmat, hst.transpose(1, 0, 2), mmat, dtile)
    y = y2.reshape(nb, nk, lc, LANES).transpose(1, 2, 0, 3).reshape(t, SSM_WIDTH)
    return y, fin[:, :hw].reshape(g, p), fin[:, hw:].reshape(g, p)


def _ssm_sample_kernel(u_ref, hr_ref, hi_ref, are_ref, aim_ref, bbr_ref, bbi_ref,
                       cr_ref, ci_ref, d_ref, y_ref, fr_ref, fi_ref, *, nb, steps):
    u = u_ref[0]
    nt = (((1,), (0,)), ((), ()))
    bu_re = lax.dot_general(u, bbr_ref[0], nt, precision=_HI, preferred_element_type=F32)
    bu_im = lax.dot_general(u, bbi_ref[0], nt, precision=_HI, preferred_element_type=F32)
    a_re, a_im = are_ref[0], aim_ref[0]
    hr, hi = hr_ref[0], hi_ref[0]
    hs_re, hs_im = [], []
    for l in range(steps):
        sl = slice(l * nb, (l + 1) * nb)
        hr, hi = (a_re * hr - a_im * hi + bu_re[sl], a_re * hi + a_im * hr + bu_im[sl])
        hs_re.append(hr)
        hs_im.append(hi)
    h_re = jnp.concatenate(hs_re, axis=0).astype(BF16)
    h_im = jnp.concatenate(hs_im, axis=0).astype(BF16)
    ntt = (((1,), (1,)), ((), ()))
    y = (lax.dot_general(h_re, cr_ref[0].astype(BF16), ntt, preferred_element_type=F32)
         - lax.dot_general(h_im, ci_ref[0].astype(BF16), ntt, preferred_element_type=F32))
    y_ref[0] = y + d_ref[0] * u
    fr_ref[0] = hr
    fi_ref[0] = hi


def _ssm_sample(u_g, h0_re, h0_im, coefs, c_re, c_im, ssm_d, nb, steps):
    g, p, c = N_SSM_GROUPS, SSM_STATE, SSM_GROUP
    abar_re, abar_im = coefs[3].reshape(g, 1, p), coefs[4].reshape(g, 1, p)
    bb_re, bb_im = coefs[7].reshape(g, c, p), coefs[8].reshape(g, c, p)
    rows = steps * nb
    sds = jax.ShapeDtypeStruct
    spec = lambda a, b: pl.BlockSpec((1, a, b), lambda i: (i, 0, 0))
    return pl.pallas_call(
        functools.partial(_ssm_sample_kernel, nb=nb, steps=steps),
        grid=(g,),
        in_specs=[spec(rows, c), spec(nb, p), spec(nb, p), spec(1, p), spec(1, p),
                  spec(c, p), spec(c, p), spec(c, p), spec(c, p), spec(1, c)],
        out_specs=[spec(rows, c), spec(nb, p), spec(nb, p)],
        out_shape=[sds((g, rows, c), F32), sds((g, nb, p), F32), sds((g, nb, p), F32)],
        compiler_params=_cparams(("parallel",)),
        name="ssm_sample",
    )(u_g, h0_re, h0_im, abar_re, abar_im, bb_re, bb_im, c_re, c_im,
      ssm_d.reshape(g, 1, c))


def _layer_norm(x, g, b):
    mu = jnp.mean(x, axis=-1, keepdims=True)
    var = jnp.mean(jnp.square(x - mu), axis=-1, keepdims=True)
    return (x - mu) * lax.rsqrt(var + LN_EPS) * g + b


def _merge_kernel(o_ref, y_ref, x_ref, gw_ref, gb_ref, wo_ref, g_ref, b_ref,
                  x1_ref, x1t_ref):
    z = jax.nn.gelu(y_ref[...])
    gate = jnp.dot(z.astype(BF16), gw_ref[...], preferred_element_type=F32) + gb_ref[...]
    z = z * jax.nn.sigmoid(gate)
    mix = jnp.dot(o_ref[...].astype(BF16), wo_ref[0:ATTN_WIDTH, :], preferred_element_type=F32)
    mix = mix + jnp.dot(z.astype(BF16), wo_ref[ATTN_WIDTH:, :], preferred_element_type=F32)
    x1 = _layer_norm(DEEPNORM_ALPHA * x_ref[...] + mix, g_ref[...], b_ref[...])
    x1_ref[...] = x1
    x1t_ref[...] = x1.T.astype(BF16)


def _merge(attn_o, y_ssm, x, glu_w, glu_b, w_out, ln_g, ln_b, tm):
    t = x.shape[0]
    full = lambda r, c: pl.BlockSpec((r, c), lambda i: (0, 0))
    return pl.pallas_call(
        _merge_kernel,
        grid=(t // tm,),
        in_specs=[pl.BlockSpec((tm, ATTN_WIDTH), lambda i: (i, 0)),
                  pl.BlockSpec((tm, SSM_WIDTH), lambda i: (i, 0)),
                  pl.BlockSpec((tm, D_MODEL), lambda i: (i, 0)),
                  full(SSM_WIDTH, SSM_WIDTH), full(1, SSM_WIDTH),
                  full(D_MODEL, D_MODEL), full(1, D_MODEL), full(1, D_MODEL)],
        out_specs=[pl.BlockSpec((tm, D_MODEL), lambda i: (i, 0)),
                   pl.BlockSpec((D_MODEL, tm), lambda i: (0, i))],
        out_shape=[jax.ShapeDtypeStruct((t, D_MODEL), F32),
                   jax.ShapeDtypeStruct((D_MODEL, t), BF16)],
        compiler_params=_cparams(("parallel",)),
        name="merge_out_ln1",
    )(attn_o, y_ssm, x, glu_w, glu_b, w_out, ln_g, ln_b)


def _top_desc(vals, n):
    rows = []
    for r in range(n):
        mx = jnp.max(vals, axis=0, keepdims=True)
        rows.append(mx)
        if r + 1 < n:
            vals = jnp.where(vals == mx, -jnp.inf, vals)
    return rows


def _route_kernel(xt_ref, wq_ref, keys_ref, e1_ref, e2_ref, te_ref):
    k = PEER_TOPK
    qt = jnp.dot(wq_ref[...], xt_ref[...], preferred_element_type=F32).astype(BF16)
    for h in range(PEER_HEADS):
        tops = []
        for m in range(2):
            hm = 2 * h + m
            sc = jnp.dot(keys_ref[hm], qt[hm * PEER_NKEYS:(hm + 1) * PEER_NKEYS, :],
                         preferred_element_type=F32)
            tops.append((sc, _top_desc(sc, k + 1)))
        (sc1, a), (sc2, b) = tops
        bmat = jnp.concatenate(b[0:k], axis=0)
        cand = jnp.concatenate([a[r] + bmat for r in range(k)], axis=0)
        best = _top_desc(cand, k + 1)
        zsum = jnp.ones_like(best[0])
        for r in range(1, k):
            zsum = zsum + jnp.exp(best[r] - best[0])
        runner_up = jnp.maximum(best[k], jnp.maximum(a[k] + b[0], a[0] + b[k]))
        theta = 0.5 * (best[k - 1] + runner_up)
        e1_ref[h] = jnp.exp(sc1 - a[0]) / zsum
        e2_ref[h] = jnp.exp(sc2 - b[0])
        te_ref[h] = jnp.exp((theta - b[0]) - sc1)


def _route(x1t, wq_t, keys, tm):
    t = x1t.shape[1]
    big = pl.BlockSpec((PEER_HEADS, PEER_NKEYS, tm), lambda i: (0, 0, i))
    sds = jax.ShapeDtypeStruct((PEER_HEADS, PEER_NKEYS, t), F32)
    return pl.pallas_call(
        _route_kernel,
        grid=(t // tm,),
        in_specs=[pl.BlockSpec((D_MODEL, tm), lambda i: (0, i)),
                  pl.BlockSpec(wq_t.shape, lambda i: (0, 0)),
                  pl.BlockSpec(keys.shape, lambda i: (0, 0, 0))],
        out_specs=[big, big, big],
        out_shape=[sds, sds, sds],
        compiler_params=_cparams(("parallel",)),
        name="peer_route",
    )(x1t, wq_t, keys)


IPB = 4


def _peer_kernel(xt_ref, u_ref, vt_ref, e1_ref, e2_ref, te_ref, o_ref, *, tsub):
    eb = pl.program_id(1)

    @pl.when(eb == 0)
    def _zero():
        o_ref[...] = jnp.zeros(o_ref.shape, F32)

    n_sub = xt_ref.shape[1] // tsub
    units = [(j, ii) for j in range(n_sub) for ii in range(IPB)]
    vrows = D_MODEL // IPB

    def up(j, ii):
        return jnp.dot(u_ref[ii * PEER_NKEYS:(ii + 1) * PEER_NKEYS, :],
                       xt_ref[:, j * tsub:(j + 1) * tsub], preferred_element_type=F32)

    def down_piece(j, a, piece):
        rows = slice(piece * vrows, (piece + 1) * vrows)
        o_ref[rows, j * tsub:(j + 1) * tsub] += jnp.dot(vt_ref[rows, :], a,
                                                        preferred_element_type=F32)

    def gate(j, ii, ht):
        i = eb * IPB + ii
        cols = slice(j * tsub, (j + 1) * tsub)
        e1rows = [e1_ref[h, pl.ds(i, 1), cols] for h in range(PEER_HEADS)]
        terows = [te_ref[h, pl.ds(i, 1), cols] for h in range(PEER_HEADS)]
        a_cols = []
        for c0 in range(0, tsub, LANES):
            hl = slice(c0, c0 + LANES)
            lc = slice(j * tsub + c0, j * tsub + c0 + LANES)
            w = None
            for h in range(PEER_HEADS):
                e2 = e2_ref[h, :, lc]
                wh = jnp.where(e2 >= terows[h][:, hl], e2, 0.0) * e1rows[h][:, hl]
                w = wh if w is None else w + wh
            a_cols.append((w * jax.nn.gelu(ht[:, hl])).astype(BF16))
        return a_cols[0] if len(a_cols) == 1 else jnp.concatenate(a_cols, axis=1)

    ht_next = up(*units[0])
    a_prev, a_rows = None, []
    for n, (j, ii) in enumerate(units):
        ht = ht_next
        if n + 1 < len(units):
            ht_next = up(*units[n + 1])
        if a_prev is not None:
            down_piece(j - 1, a_prev, ii)
        a_rows.append(gate(j, ii, ht))
        if ii == IPB - 1:
            a_prev, a_rows = jnp.concatenate(a_rows, axis=0), []
    for piece in range(IPB):
        down_piece(n_sub - 1, a_prev, piece)


def _peer(x1t, u_tab, vt_tab, e1, e2, te_thr, tm, tsub):
    t = x1t.shape[1]
    te = IPB * PEER_NKEYS
    n_eb = u_tab.shape[0] // te
    once = pl.Buffered(1)
    big = pl.BlockSpec((PEER_HEADS, PEER_NKEYS, tm), lambda i, e: (0, 0, i), pipeline_mode=once)
    return pl.pallas_call(
        functools.partial(_peer_kernel, tsub=tsub),
        grid=(t // tm, n_eb),
        in_specs=[pl.BlockSpec((D_MODEL, tm), lambda i, e: (0, i), pipeline_mode=once),
                  pl.BlockSpec((te, D_MODEL), lambda i, e: (e, 0)),
                  pl.BlockSpec((D_MODEL, te), lambda i, e: (0, e)),
                  big, big, big],
        out_specs=pl.BlockSpec((D_MODEL, tm), lambda i, e: (0, i)),
        out_shape=jax.ShapeDtypeStruct((D_MODEL, t), F32),
        compiler_params=_cparams(("parallel", "arbitrary")),
        name="peer_experts",
    )(x1t, u_tab, vt_tab, e1, e2, te_thr)


def _ln2_kernel(x1_ref, ft_ref, g_ref, b_ref, o_ref):
    o_ref[...] = _layer_norm(DEEPNORM_ALPHA * x1_ref[...] + ft_ref[...].T,
                             g_ref[...], b_ref[...])


def _ln2(x1, ffn_t, ln_g, ln_b, tm):
    t = x1.shape[0]
    return pl.pallas_call(
        _ln2_kernel,
        grid=(t // tm,),
        in_specs=[pl.BlockSpec((tm, D_MODEL), lambda i: (i, 0)),
                  pl.BlockSpec((D_MODEL, tm), lambda i: (0, i)),
                  pl.BlockSpec((1, D_MODEL), lambda i: (0, 0)),
                  pl.BlockSpec((1, D_MODEL), lambda i: (0, 0))],
        out_specs=pl.BlockSpec((tm, D_MODEL), lambda i: (i, 0)),
        out_shape=jax.ShapeDtypeStruct((t, D_MODEL), F32),
        compiler_params=_cparams(("parallel",)),
        name="ln2",
    )(x1, ffn_t, ln_g, ln_b)


def _prompt_buckets(tb):
    r = np.arange(tb)[:, None]
    c = np.arange(tb)[None, :]
    diag = np.where(c <= r, _bucket_np(r - c), -1)
    sub = _bucket_np(r - c + tb)
    return np.stack([diag, sub]).astype(np.int32)


def _sample_buckets(past_len, dec_seq):
    qpos = past_len + np.tile(np.arange(dec_seq), 2)[:, None]
    w = PPS * PAGE_SIZE
    kpos = (past_len - w) + np.arange(w)[None, :]
    last = _bucket_np(qpos - kpos)
    jn = np.arange(NEWK)[None, :]
    knew = past_len + jn
    new = np.where((jn < dec_seq) & (knew <= qpos), _bucket_np(qpos - knew), -1)
    return last[None].astype(np.int32), new[None].astype(np.int32)


def _group_tail(x1, x1t, w_query_t, keys, u_tab, vt_tab, ln_g, ln_b, tm_route, tm_peer, tm_ln):
    e1, e2, te_thr = _route(x1t, w_query_t, keys, tm_route)
    ffn_t = _peer(x1t, u_tab, vt_tab, e1, e2, te_thr, tm_peer, min(tm_peer, 256))
    return _ln2(x1, ffn_t, ln_g, ln_b, tm_ln)


def kernel(x_prompt, x_sample, cache_k, cache_v, state_ssm_re, state_ssm_im, page_table, w_in, lambda_q1, lambda_k1, lambda_q2, lambda_k2, attn_subln_g, rel_bias, ssm_a_re, ssm_a_im, ssm_b_re, ssm_b_im, ssm_c_re, ssm_c_im, ssm_d, ssm_log_dt, ssm_glu_w, ssm_glu_b, w_out, ln1_g, ln1_b, peer_w_query, peer_sub_keys, peer_u, peer_v, ln2_g, ln2_b):
    l = 0
    seq = x_prompt.shape[1]
    dec_b, dec_s = x_sample.shape[0], x_sample.shape[1]
    n_pages = page_table.shape[1]
    past_len = n_pages * PAGE_SIZE
    g, p, c = N_SSM_GROUPS, SSM_STATE, SSM_GROUP
    lam_init = 0.8 - 0.6 * math.exp(-0.3 * l)
    lam = (jnp.exp(jnp.sum(lambda_q1[l] * lambda_k1[l]))
           - jnp.exp(jnp.sum(lambda_q2[l] * lambda_k2[l])) + lam_init).reshape(1).astype(F32)

    w_in_b = w_in[l].astype(BF16)
    glu_w_b = ssm_glu_w[l].astype(BF16)
    w_out_b = w_out[l].astype(BF16)
    wq_t = peer_w_query[l].T.astype(BF16)
    keys = peer_sub_keys[l].reshape(2 * PEER_HEADS, PEER_NKEYS, PEER_NKEYS).astype(BF16)
    u_tab = peer_u[l].astype(BF16)
    vt_tab = peer_v[l].T.astype(BF16)
    subln_g = attn_subln_g[l].reshape(1, HEAD_DIM)
    glu_b = ssm_glu_b[l].reshape(1, SSM_WIDTH)
    g1, b1 = ln1_g[l].reshape(1, D_MODEL), ln1_b[l].reshape(1, D_MODEL)
    g2, b2 = ln2_g[l].reshape(1, D_MODEL), ln2_b[l].reshape(1, D_MODEL)
    coefs = _ssm_coefs(ssm_log_dt[l], ssm_a_re[l], ssm_a_im[l],
                       ssm_b_re[l].transpose(0, 2, 1), ssm_b_im[l].transpose(0, 2, 1),
                       ssm_c_re[l], ssm_c_im[l])

    xp = x_prompt.reshape(seq, D_MODEL)
    tb = 512
    qkvu = _proj(xp, w_in_b, 512, ATTN_WIDTH)
    bias_p = _bias_tiles(rel_bias, _prompt_buckets(tb))
    attn_p = _attn_prompt(qkvu, bias_p, lam, subln_g, tb, lam_init)
    y_p, hr_p, hi_p = _ssm_prompt(qkvu[3], coefs, ssm_d[l])
    x1_p, x1t_p = _merge(attn_p, y_p, xp, glu_w_b, glu_b, w_out_b, g1, b1, 256)
    out_p = _group_tail(x1_p, x1t_p, wq_t, keys, u_tab, vt_tab, g2, b2, 256, 1024, 256)

    ts = dec_b * dec_s
    xs = x_sample.reshape(ts, D_MODEL)
    qkvu_s = _proj(xs, w_in_b, ts, ATTN_WIDTH)
    q_s = qkvu_s[0].reshape(dec_b, dec_s, N_HEADS, HEAD_DIM) * (HALF_DIM ** -0.5)
    lane = np.arange(HEAD_DIM)
    halves = jnp.asarray(np.stack([lane < HALF_DIM, lane >= HALF_DIM]).astype(np.float32))
    q2 = (q_s[:, None] * halves[None, :, None, None, :])
    q2 = q2.transpose(0, 3, 1, 2, 4).reshape(dec_b, N_HEADS, QROWS, HEAD_DIM)
    pad_new = lambda a: jnp.pad(
        a.reshape(dec_b, dec_s, N_HEADS, HEAD_DIM).transpose(0, 2, 1, 3),
        ((0, 0), (0, 0), (0, NEWK - dec_s), (0, 0)))
    bl, bn = _sample_buckets(past_len, dec_s)
    bias_last = _bias_tiles(rel_bias, bl)[:, 0]
    bias_new = _bias_tiles(rel_bias, bn)[:, 0]
    attn_s = _attn_sample(q2, pad_new(qkvu_s[1]), pad_new(qkvu_s[2]), cache_k[l], cache_v[l],
                          page_table, bias_last, bias_new, lam, subln_g, lam_init)
    u_s = qkvu_s[3].reshape(dec_b, dec_s, g, c).transpose(2, 1, 0, 3).reshape(g, ts, c)
    y_g, hr_s, hi_s = _ssm_sample(u_s, state_ssm_re[l].transpose(1, 0, 2),
                                  state_ssm_im[l].transpose(1, 0, 2), coefs,
                                  ssm_c_re[l], ssm_c_im[l], ssm_d[l], dec_b, dec_s)
    y_s = y_g.reshape(g, dec_s, dec_b, c).transpose(2, 1, 0, 3).reshape(ts, SSM_WIDTH)
    x1_s, x1t_s = _merge(attn_s.reshape(ts, ATTN_WIDTH), y_s, xs, glu_w_b, glu_b, w_out_b,
                         g1, b1, ts)
    out_s = _group_tail(x1_s, x1t_s, wq_t, keys, u_tab, vt_tab, g2, b2, ts, ts, ts)

    kv = lambda a, b_, s_: a.reshape(1, b_, s_, N_HEADS, HEAD_DIM)
    return (out_p.reshape(1, seq, D_MODEL), out_s.reshape(dec_b, dec_s, D_MODEL),
            kv(qkvu[1], 1, seq), kv(qkvu[2], 1, seq),
            hr_p.reshape(1, 1, g, p), hi_p.reshape(1, 1, g, p),
            kv(qkvu_s[1], dec_b, dec_s), kv(qkvu_s[2], dec_b, dec_s),
            hr_s.transpose(1, 0, 2)[None], hi_s.transpose(1, 0, 2)[None])
```

```python
import functools
import math

import numpy as np
import jax
import jax.numpy as jnp
from jax import lax
from jax.experimental import pallas as pl
from jax.experimental.pallas import tpu as pltpu

F32 = jnp.float32
BF16 = jnp.bfloat16

D_MODEL = 2048
PAGE_SIZE = 128
ATTN_WIDTH = 1024
SSM_WIDTH = 1024
N_HEADS = 8
HEAD_DIM = 128
HALF_DIM = 64
N_BUCKETS = 32
MAX_DISTANCE = 128
SSM_GROUP = 16
N_SSM_GROUPS = 64
SSM_STATE = 64
PEER_HEADS = 8
PEER_NKEYS = 128
PEER_TOPK = 16
LN_EPS = 1e-5
DEPTH = 1
DEEPNORM_ALPHA = (2.0 * DEPTH) ** 0.25

LANES = 128
SCAN_CHUNK = 16
GBLK = LANES // SSM_GROUP
NEG = -1e30
LOG2E = 1.4426950408889634
VMEM_LIMIT = 56 * 1024 * 1024

_HI = lax.Precision.HIGHEST


def _cparams(sem):
    return pltpu.CompilerParams(dimension_semantics=sem, vmem_limit_bytes=VMEM_LIMIT)


def _rep(x, n):
    return x if n == 1 else jnp.concatenate([x] * n, axis=1)


def _proj_kernel(x_ref, w_ref, o_ref):
    o_ref[0] = jnp.dot(x_ref[...].astype(BF16), w_ref[...], preferred_element_type=F32)


def _proj(x, w, tm, tn):
    t, k = x.shape
    n = w.shape[1]
    return pl.pallas_call(
        _proj_kernel,
        grid=(t // tm, n // tn),
        in_specs=[pl.BlockSpec((tm, k), lambda i, j: (i, 0)),
                  pl.BlockSpec((k, tn), lambda i, j: (0, j))],
        out_specs=pl.BlockSpec((1, tm, tn), lambda i, j: (j, i, 0)),
        out_shape=jax.ShapeDtypeStruct((n // tn, t, tn), F32),
        compiler_params=_cparams(("parallel", "arbitrary")),
        name="in_proj",
    )(x, w)


def _bucket_np(d):
    n = np.maximum(d, 0)
    max_exact = N_BUCKETS // 2
    nf = np.maximum(n, 1).astype(np.float64)
    large = max_exact + (np.log(nf / max_exact) / math.log(MAX_DISTANCE / max_exact)
                         * (N_BUCKETS - max_exact)).astype(np.int32)
    large = np.minimum(large, N_BUCKETS - 1)
    return np.where(n < max_exact, n, large).astype(np.int32)


def _bias_kernel(rb_ref, bkt_ref, o_ref, *, scale):
    h = pl.program_id(0)
    bkt = bkt_ref[0]
    far = rb_ref[N_BUCKETS - 1, h]
    acc = jnp.where(bkt < 0, NEG, 0.0).astype(F32)
    for b in range(N_BUCKETS - 1):
        acc = jnp.where(bkt == b, (rb_ref[b, h] - far) * scale, acc)
    o_ref[0, 0] = acc


def _bias_tiles(rel_bias, bkt, scale=1.0):
    n, r, c = bkt.shape
    return pl.pallas_call(
        functools.partial(_bias_kernel, scale=scale),
        grid=(N_HEADS, n),
        in_specs=[pl.BlockSpec(memory_space=pltpu.SMEM),
                  pl.BlockSpec((1, r, c), lambda h, i: (i, 0, 0))],
        out_specs=pl.BlockSpec((1, 1, r, c), lambda h, i: (h, i, 0, 0)),
        out_shape=jax.ShapeDtypeStruct((N_HEADS, n, r, c), F32),
        compiler_params=_cparams(("parallel", "arbitrary")),
        name="bias_tiles",
    )(rel_bias, jnp.asarray(bkt))


def _subln(attn, g_row, lam_init):
    ms = jnp.mean(jnp.square(attn), axis=-1, keepdims=True)
    return attn * lax.rsqrt(ms + LN_EPS) * g_row * (1.0 - lam_init)


def _attn_kernel(qi_ref, ki_ref, lam_ref, q_ref, k_ref, v_ref, bias_ref, g_ref, o_ref,
                 q2_sc, m_sc, l_sc, acc_sc, *, tb, lam_init):
    s = pl.program_id(1)
    qi = qi_ref[s]
    ki = ki_ref[s]

    @pl.when(ki == 0)
    def _init():
        q = q_ref[0] * (HALF_DIM ** -0.5 * LOG2E)
        lane = lax.broadcasted_iota(jnp.int32, q.shape, 1)
        q2_sc[0:tb, :] = jnp.where(lane < HALF_DIM, q, 0.0).astype(BF16)
        q2_sc[tb:2 * tb, :] = jnp.where(lane >= HALF_DIM, q, 0.0).astype(BF16)
        m_sc[...] = jnp.full(m_sc.shape, NEG, F32)
        l_sc[...] = jnp.zeros(l_sc.shape, F32)
        acc_sc[...] = jnp.zeros(acc_sc.shape, F32)

    def logits():
        kb = k_ref[0].astype(BF16)
        return lax.dot_general(q2_sc[...], kb, (((1,), (1,)), ((), ())),
                               preferred_element_type=F32)

    def update(sc):
        m_prev = m_sc[...]
        m_new = jnp.maximum(m_prev, jnp.max(sc, axis=-1, keepdims=True))
        alpha = jnp.exp2(m_prev - m_new)
        p = jnp.exp2(sc - _rep(m_new, tb // LANES))
        l_sc[...] = alpha * l_sc[...] + jnp.sum(p, axis=-1, keepdims=True)
        acc_sc[...] = alpha * acc_sc[...] + jnp.dot(
            p.astype(BF16), v_ref[0].astype(BF16), preferred_element_type=F32)
        m_sc[...] = m_new

    @pl.when(ki < qi - 1)
    def _far():
        update(logits())

    @pl.when(ki == qi - 1)
    def _sub():
        b = bias_ref[0, 1]
        update(logits() + jnp.concatenate([b, b], axis=0))

    @pl.when(ki == qi)
    def _diag():
        b = bias_ref[0, 0]
        update(logits() + jnp.concatenate([b, b], axis=0))
        o = acc_sc[...] / l_sc[...]
        attn = o[0:tb] - lam_ref[0] * o[tb:2 * tb]
        o_ref[...] = _subln(attn, g_ref[...], lam_init)


def _attn_prompt(qkvu, bias, lam, subln_g, tb, lam_init):
    t = qkvu.shape[1]
    nb = t // tb
    qi_idx = np.concatenate([np.full(i + 1, i) for i in range(nb)]).astype(np.int32)
    ki_idx = np.concatenate([np.arange(i + 1) for i in range(nb)]).astype(np.int32)
    grid_spec = pltpu.PrefetchScalarGridSpec(
        num_scalar_prefetch=2,
        grid=(N_HEADS, len(qi_idx)),
        in_specs=[
            pl.BlockSpec(memory_space=pltpu.SMEM),
            pl.BlockSpec((1, tb, HEAD_DIM), lambda h, s, qi, ki: (0, qi[s], h)),
            pl.BlockSpec((1, tb, HEAD_DIM), lambda h, s, qi, ki: (1, ki[s], h)),
            pl.BlockSpec((1, tb, HEAD_DIM), lambda h, s, qi, ki: (2, ki[s], h)),
            pl.BlockSpec((1, 2, tb, tb), lambda h, s, qi, ki: (h, 0, 0, 0)),
            pl.BlockSpec((1, HEAD_DIM), lambda h, s, qi, ki: (0, 0)),
        ],
        out_specs=pl.BlockSpec((tb, HEAD_DIM), lambda h, s, qi, ki: (qi[s], h)),
        scratch_shapes=[
            pltpu.VMEM((2 * tb, HEAD_DIM), BF16),
            pltpu.VMEM((2 * tb, LANES), F32),
            pltpu.VMEM((2 * tb, LANES), F32),
            pltpu.VMEM((2 * tb, HEAD_DIM), F32),
        ],
    )
    return pl.pallas_call(
        functools.partial(_attn_kernel, tb=tb, lam_init=lam_init),
        grid_spec=grid_spec,
        out_shape=jax.ShapeDtypeStruct((t, ATTN_WIDTH), F32),
        compiler_params=_cparams(("parallel", "arbitrary")),
        name="attn_prompt",
    )(jnp.asarray(qi_idx), jnp.asarray(ki_idx), lam, qkvu, qkvu, qkvu, bias, subln_g)


PPS = 8
QROWS = 8
NEWK = 16


def _decode_kernel(pt_ref, lam_ref, q2_ref, kn_ref, vn_ref, bl_ref, bn_ref, g_ref, *rest,
                   n_steps, lam_init):
    k_refs = rest[0:PPS]
    v_refs = rest[PPS:2 * PPS]
    o_ref = rest[2 * PPS]
    m_sc, l_sc, acc_sc = rest[2 * PPS + 1:]
    p_idx = pl.program_id(1)
    last = p_idx == n_steps - 1

    @pl.when(p_idx == 0)
    def _init():
        m_sc[...] = jnp.full(m_sc.shape, NEG, F32)
        l_sc[...] = jnp.zeros(l_sc.shape, F32)
        acc_sc[...] = jnp.zeros(acc_sc.shape, F32)

    last_f = jnp.where(last, 1.0, 0.0).astype(F32)

    def softmax_step(h, sc):
        m_prev = m_sc[h]
        m_new = jnp.maximum(m_prev, jnp.max(sc, axis=-1, keepdims=True))
        alpha = jnp.exp(m_prev - m_new)
        p = jnp.exp(sc - _rep(m_new, sc.shape[1] // LANES) if sc.shape[1] >= LANES
                    else sc - m_new[:, 0:sc.shape[1]])
        l_sc[h] = alpha * l_sc[h] + jnp.sum(p, axis=-1, keepdims=True)
        m_sc[h] = m_new
        return p.astype(BF16), alpha

    def accumulate(h, p, alpha, vmat):
        acc_sc[h] = alpha * acc_sc[h] + jnp.dot(p, vmat, preferred_element_type=F32)

    def update(h, sc, vmat):
        p, alpha = softmax_step(h, sc)
        accumulate(h, p, alpha, vmat)

    def head_rows(refs, h):
        rows_h = pl.ds(h, PAGE_SIZE, stride=N_HEADS)
        return jnp.concatenate([r[0, rows_h, :] for r in refs], axis=0).astype(BF16)

    scores = []
    for h in range(N_HEADS):
        q2 = q2_ref[0, h].astype(BF16)
        sc = lax.dot_general(q2, head_rows(k_refs, h), (((1,), (1,)), ((), ())),
                             preferred_element_type=F32)
        scores.append(sc + bl_ref[h] * last_f)
    probs = [softmax_step(h, scores[h]) for h in range(N_HEADS)]
    for h in range(N_HEADS):
        accumulate(h, probs[h][0], probs[h][1], head_rows(v_refs, h))

    @pl.when(last)
    def _finish():
        for h in range(N_HEADS):
            q2 = q2_ref[0, h].astype(BF16)
            kn = kn_ref[0, h].astype(BF16)
            vn = vn_ref[0, h].astype(BF16)
            sc = lax.dot_general(q2, kn, (((1,), (1,)), ((), ())), preferred_element_type=F32)
            update(h, sc + bn_ref[h], vn)
            o = acc_sc[h] / l_sc[h]
            half = QROWS // 2
            attn = o[0:half] - lam_ref[0] * o[half:QROWS]
            o_ref[0, :, h * HEAD_DIM:(h + 1) * HEAD_DIM] = _subln(attn, g_ref[...], lam_init)


def _attn_sample(q2, k_new, v_new, cache_k, cache_v, page_table, bias_last, bias_new,
                 lam, subln_g, lam_init):
    nb, n_pages = page_table.shape
    n_steps = n_pages // PPS
    dec_seq = QROWS // 2
    pages = lambda cache: cache.reshape(cache.shape[0], PAGE_SIZE * N_HEADS, HEAD_DIM)

    def page_spec(j):
        return pl.BlockSpec(
            (1, PAGE_SIZE * N_HEADS, HEAD_DIM),
            lambda b, p, pt, j=j: (pt[b * n_pages + p * PPS + j], 0, 0))

    grid_spec = pltpu.PrefetchScalarGridSpec(
        num_scalar_prefetch=1,
        grid=(nb, n_steps),
        in_specs=[
            pl.BlockSpec(memory_space=pltpu.SMEM),
            pl.BlockSpec((1, N_HEADS, QROWS, HEAD_DIM), lambda b, p, pt: (b, 0, 0, 0)),
            pl.BlockSpec((1, N_HEADS, NEWK, HEAD_DIM), lambda b, p, pt: (b, 0, 0, 0)),
            pl.BlockSpec((1, N_HEADS, NEWK, HEAD_DIM), lambda b, p, pt: (b, 0, 0, 0)),
            pl.BlockSpec((N_HEADS, QROWS, PPS * PAGE_SIZE), lambda b, p, pt: (0, 0, 0)),
            pl.BlockSpec((N_HEADS, QROWS, NEWK), lambda b, p, pt: (0, 0, 0)),
            pl.BlockSpec((1, HEAD_DIM), lambda b, p, pt: (0, 0)),
        ] + [page_spec(j) for j in range(PPS)] + [page_spec(j) for j in range(PPS)],
        out_specs=pl.BlockSpec((1, dec_seq, ATTN_WIDTH), lambda b, p, pt: (b, 0, 0)),
        scratch_shapes=[
            pltpu.VMEM((N_HEADS, QROWS, LANES), F32),
            pltpu.VMEM((N_HEADS, QROWS, LANES), F32),
            pltpu.VMEM((N_HEADS, QROWS, HEAD_DIM), F32),
        ],
    )
    return pl.pallas_call(
        functools.partial(_decode_kernel, n_steps=n_steps, lam_init=lam_init),
        grid_spec=grid_spec,
        out_shape=jax.ShapeDtypeStruct((nb, dec_seq, ATTN_WIDTH), F32),
        compiler_params=_cparams(("parallel", "arbitrary")),
        name="attn_sample",
    )(page_table.reshape(-1), lam, q2, k_new, v_new, bias_last, bias_new, subln_g,
      *([pages(cache_k)] * PPS), *([pages(cache_v)] * PPS))


def _ssm_coef_kernel(ldt_ref, ar_ref, ai_ref, bt_re_ref, bt_im_ref, c_re_ref, c_im_ref,
                     t_ref, s_ref, m_ref,
                     abar_re_ref, abar_im_ref, apow_re_ref, apow_im_ref,
                     bb_re_ref, bb_im_ref):
    lc, c, p = SCAN_CHUNK, SSM_GROUP, SSM_STATE
    hw = GBLK * p
    dt = jnp.exp(ldt_ref[0])
    ar, ai = ar_ref[0], ai_ref[0]
    mag = jnp.exp(ar * dt)
    abar_re = mag * jnp.cos(ai * dt)
    abar_im = mag * jnp.sin(ai * dt)
    nr, ni = abar_re - 1.0, abar_im
    den = ar * ar + ai * ai
    f_re = ((nr * ar + ni * ai) / den)[:, None, :]
    f_im = ((ni * ar - nr * ai) / den)[:, None, :]
    bt_re, bt_im = bt_re_ref[0], bt_im_ref[0]
    bb_re = f_re * bt_re - f_im * bt_im
    bb_im = f_re * bt_im + f_im * bt_re
    c_re, c_im = c_re_ref[0], c_im_ref[0]

    pw = [(jnp.ones_like(abar_re), jnp.zeros_like(abar_re))]
    for _ in range(lc):
        pr, pi = pw[-1]
        pw.append((pr * abar_re - pi * abar_im, pr * abar_im + pi * abar_re))

    rid = lax.broadcasted_iota(jnp.int32, (LANES, hw), 0) // c
    cid = lax.broadcasted_iota(jnp.int32, (LANES, hw), 1) // p
    same_group = rid == cid

    def bdiag(x):
        flat = x.reshape(LANES, p)
        return jnp.where(same_group, jnp.concatenate([flat] * GBLK, axis=1), 0.0)

    def cmul(xr, xi, w):
        wr, wi = w[0][:, None, :], w[1][:, None, :]
        return xr * wr - xi * wi, xr * wi + xi * wr

    l_re, l_im = bdiag(bb_re), bdiag(bb_im)
    nt = (((1,), (1,)), ((), ()))
    zero_tile = jnp.zeros((LANES, LANES), BF16)
    bd = []
    for tau in range(lc + 1):
        cr, ci = cmul(c_re, c_im, pw[tau])
        r_re, r_im = bdiag(cr), bdiag(ci)
        if tau < lc:
            bd.append((lax.dot_general(l_re, r_re, nt, precision=_HI, preferred_element_type=F32)
                       - lax.dot_general(l_im, r_im, nt, precision=_HI,
                                         preferred_element_type=F32)).astype(BF16))
        if tau > 0:
            cols = slice((tau - 1) * LANES, tau * LANES)
            m_ref[0, 0:hw, cols] = r_re.T.astype(BF16)
            m_ref[0, hw:2 * hw, cols] = (-r_im).T.astype(BF16)
    for s in range(lc):
        rows = slice(s * LANES, (s + 1) * LANES)
        for r in range(lc):
            t_ref[0, rows, r * LANES:(r + 1) * LANES] = bd[r - s] if r >= s else zero_tile
        wr, wi = cmul(bb_re, bb_im, pw[lc - 1 - s])
        s_ref[0, rows, 0:hw] = bdiag(wr).astype(BF16)
        s_ref[0, rows, hw:2 * hw] = bdiag(wi).astype(BF16)
    abar_re_ref[0] = abar_re
    abar_im_ref[0] = abar_im
    apow_re_ref[0] = pw[lc][0]
    apow_im_ref[0] = pw[lc][1]
    bb_re_ref[0] = bb_re
    bb_im_ref[0] = bb_im


def _ssm_coefs(log_dt, a_re, a_im, bt_re, bt_im, c_re, c_im):
    g, p, c, lc = N_SSM_GROUPS, SSM_STATE, SSM_GROUP, SCAN_CHUNK
    nb, hw, w = g // GBLK, GBLK * p, lc * LANES
    vec = pl.BlockSpec((1, GBLK, p), lambda i: (i, 0, 0))
    mat = pl.BlockSpec((1, GBLK, c, p), lambda i: (i, 0, 0, 0))
    sds = jax.ShapeDtypeStruct
    blk = lambda a: a.reshape((nb, GBLK) + a.shape[1:])
    return pl.pallas_call(
        _ssm_coef_kernel,
        grid=(nb,),
        in_specs=[pl.BlockSpec((1, GBLK, 1), lambda i: (i, 0, 0)), vec, vec, mat, mat, mat, mat],
        out_specs=[pl.BlockSpec((1, w, w), lambda i: (i, 0, 0)),
                   pl.BlockSpec((1, w, 2 * hw), lambda i: (i, 0, 0)),
                   pl.BlockSpec((1, 2 * hw, w), lambda i: (i, 0, 0)),
                   vec, vec, vec, vec, mat, mat],
        out_shape=[sds((nb, w, w), BF16), sds((nb, w, 2 * hw), BF16), sds((nb, 2 * hw, w), BF16)]
        + [sds((nb, GBLK, p), F32)] * 4 + [sds((nb, GBLK, c, p), F32)] * 2,
        compiler_params=_cparams(("parallel",)),
        name="ssm_coefs",
    )(log_dt.reshape(nb, GBLK, 1), blk(a_re), blk(a_im), blk(bt_re), blk(bt_im),
      blk(c_re), blk(c_im))


def _ssm_local_kernel(u_ref, s_ref, o_ref):
    o_ref[0] = jnp.dot(u_ref[0].astype(BF16), s_ref[0], preferred_element_type=F32)


def _ssm_local(u2, smat):
    gp, nk, w = u2.shape
    n = smat.shape[2]
    return pl.pallas_call(
        _ssm_local_kernel,
        grid=(gp,),
        in_specs=[pl.BlockSpec((1, nk, w), lambda i: (i, 0, 0)),
                  pl.BlockSpec((1, w, n), lambda i: (i, 0, 0))],
        out_specs=pl.BlockSpec((1, nk, n), lambda i: (i, 0, 0)),
        out_shape=jax.ShapeDtypeStruct((gp, nk, n), F32),
        compiler_params=_cparams(("parallel",)),
        name="ssm_local",
    )(u2, smat)


def _ssm_rec_kernel(s_ref, are_ref, aim_ref, h_ref, fin_ref, *, nk):
    w = are_ref.shape[1]
    a_re = are_ref[...]
    a_im = aim_ref[...]

    def body(k, carry):
        hr, hi = carry
        h_ref[k] = jnp.concatenate([hr, hi], axis=1)
        sk = s_ref[k]
        return (a_re * hr - a_im * hi + sk[:, 0:w], a_re * hi + a_im * hr + sk[:, w:2 * w])

    z = jnp.zeros(a_re.shape, F32)
    hr, hi = lax.fori_loop(0, nk, body, (z, z))
    fin_ref[...] = jnp.concatenate([hr, hi], axis=1)


def _ssm_rec(s_kg, apow_re, apow_im):
    nk, gp, w2 = s_kg.shape
    return pl.pallas_call(
        functools.partial(_ssm_rec_kernel, nk=nk),
        out_shape=[jax.ShapeDtypeStruct((nk, gp, w2), F32),
                   jax.ShapeDtypeStruct((gp, w2), F32)],
        compiler_params=pltpu.CompilerParams(vmem_limit_bytes=VMEM_LIMIT),
        name="ssm_rec",
    )(s_kg, apow_re, apow_im)


def _ssm_out_kernel(u_ref, uh_ref, t_ref, h_ref, m_ref, d_ref, o_ref):
    y = jnp.dot(u_ref[0].astype(BF16), t_ref[0], preferred_element_type=F32)
    y = y + jnp.dot(h_ref[0].astype(BF16), m_ref[0], preferred_element_type=F32)
    o_ref[0] = y + d_ref[0] * uh_ref[0]


def _ssm_out(u2, tmat, hst, mmat, dtile):
    nb, nk, w = u2.shape
    hw = hst.shape[2]
    nh = 2
    wn = w // nh
    return pl.pallas_call(
        _ssm_out_kernel,
        grid=(nb, nh),
        in_specs=[pl.BlockSpec((1, nk, w), lambda i, j: (i, 0, 0)),
                  pl.BlockSpec((1, nk, wn), lambda i, j: (i, 0, j)),
                  pl.BlockSpec((1, w, wn), lambda i, j: (i, 0, j)),
                  pl.BlockSpec((1, nk, hw), lambda i, j: (i, 0, 0)),
                  pl.BlockSpec((1, hw, wn), lambda i, j: (i, 0, j)),
                  pl.BlockSpec((1, 1, wn), lambda i, j: (i, 0, j))],
        out_specs=pl.BlockSpec((1, nk, wn), lambda i, j: (i, 0, j)),
        out_shape=jax.ShapeDtypeStruct((nb, nk, w), F32),
        compiler_params=_cparams(("parallel", "arbitrary")),
        name="ssm_out",
    )(u2, u2, tmat, hst, mmat, dtile)


def _ssm_prompt(u, coefs, ssm_d):
    tmat, smat, mmat, _, _, apow_re, apow_im, _, _ = coefs
    t = u.shape[0]
    g, p, lc = N_SSM_GROUPS, SSM_STATE, SCAN_CHUNK
    nk, nb = t // lc, g // GBLK
    hw = GBLK * p
    u2 = u.reshape(nk, lc, nb, LANES).transpose(2, 0, 1, 3).reshape(nb, nk, lc * LANES)
    s_loc = _ssm_local(u2, smat)
    hst, fin = _ssm_rec(s_loc.transpose(1, 0, 2), apow_re.reshape(nb, hw),
                        apow_im.reshape(nb, hw))
    dtile = jnp.tile(ssm_d.reshape(nb, 1, LANES), (1, 1, lc))
    y2 = _ssm_out(u2, tmat, hst.transpose(1, 0, 2), mmat, dtile)
    y = y2.reshape(nb, nk, lc, LANES).transpose(1, 2, 0, 3).reshape(t, SSM_WIDTH)
    return y, fin[:, :hw].reshape(g, p), fin[:, hw:].reshape(g, p)


def _ssm_sample_kernel(u_ref, hr_ref, hi_ref, are_ref, aim_ref, bbr_ref, bbi_ref,
                       cr_ref, ci_ref, d_ref, y_ref, fr_ref, fi_ref, *, nb, steps):
    u = u_ref[0]
    nt = (((1,), (0,)), ((), ()))
    bu_re = lax.dot_general(u, bbr_ref[0], nt, precision=_HI, preferred_element_type=F32)
    bu_im = lax.dot_general(u, bbi_ref[0], nt, precision=_HI, preferred_element_type=F32)
    a_re, a_im = are_ref[0], aim_ref[0]
    hr, hi = hr_ref[0], hi_ref[0]
    hs_re, hs_im = [], []
    for l in range(steps):
        sl = slice(l * nb, (l + 1) * nb)
        hr, hi = (a_re * hr - a_im * hi + bu_re[sl], a_re * hi + a_im * hr + bu_im[sl])
        hs_re.append(hr)
        hs_im.append(hi)
    h_re = jnp.concatenate(hs_re, axis=0).astype(BF16)
    h_im = jnp.concatenate(hs_im, axis=0).astype(BF16)
    ntt = (((1,), (1,)), ((), ()))
    y = (lax.dot_general(h_re, cr_ref[0].astype(BF16), ntt, preferred_element_type=F32)
         - lax.dot_general(h_im, ci_ref[0].astype(BF16), ntt, preferred_element_type=F32))
    y_ref[0] = y + d_ref[0] * u
    fr_ref[0] = hr
    fi_ref[0] = hi


def _ssm_sample(u_g, h0_re, h0_im, coefs, c_re, c_im, ssm_d, nb, steps):
    g, p, c = N_SSM_GROUPS, SSM_STATE, SSM_GROUP
    abar_re, abar_im = coefs[3].reshape(g, 1, p), coefs[4].reshape(g, 1, p)
    bb_re, bb_im = coefs[7].reshape(g, c, p), coefs[8].reshape(g, c, p)
    rows = steps * nb
    sds = jax.ShapeDtypeStruct
    spec = lambda a, b: pl.BlockSpec((1, a, b), lambda i: (i, 0, 0))
    return pl.pallas_call(
        functools.partial(_ssm_sample_kernel, nb=nb, steps=steps),
        grid=(g,),
        in_specs=[spec(rows, c), spec(nb, p), spec(nb, p), spec(1, p), spec(1, p),
                  spec(c, p), spec(c, p), spec(c, p), spec(c, p), spec(1, c)],
        out_specs=[spec(rows, c), spec(nb, p), spec(nb, p)],
        out_shape=[sds((g, rows, c), F32), sds((g, nb, p), F32), sds((g, nb, p), F32)],
        compiler_params=_cparams(("parallel",)),
        name="ssm_sample",
    )(u_g, h0_re, h0_im, abar_re, abar_im, bb_re, bb_im, c_re, c_im,
      ssm_d.reshape(g, 1, c))


def _layer_norm(x, g, b):
    mu = jnp.mean(x, axis=-1, keepdims=True)
    var = jnp.mean(jnp.square(x - mu), axis=-1, keepdims=True)
    return (x - mu) * lax.rsqrt(var + LN_EPS) * g + b


def _merge_kernel(o_ref, y_ref, x_ref, gw_ref, gb_ref, wo_ref, g_ref, b_ref,
                  x1_ref, x1t_ref):
    z = jax.nn.gelu(y_ref[...])
    gate = jnp.dot(z.astype(BF16), gw_ref[...], preferred_element_type=F32) + gb_ref[...]
    z = z * jax.nn.sigmoid(gate)
    mix = jnp.dot(o_ref[...].astype(BF16), wo_ref[0:ATTN_WIDTH, :], preferred_element_type=F32)
    mix = mix + jnp.dot(z.astype(BF16), wo_ref[ATTN_WIDTH:, :], preferred_element_type=F32)
    x1 = _layer_norm(DEEPNORM_ALPHA * x_ref[...] + mix, g_ref[...], b_ref[...])
    x1_ref[...] = x1
    x1t_ref[...] = x1.T.astype(BF16)


def _merge(attn_o, y_ssm, x, glu_w, glu_b, w_out, ln_g, ln_b, tm):
    t = x.shape[0]
    full = lambda r, c: pl.BlockSpec((r, c), lambda i: (0, 0))
    return pl.pallas_call(
        _merge_kernel,
        grid=(t // tm,),
        in_specs=[pl.BlockSpec((tm, ATTN_WIDTH), lambda i: (i, 0)),
                  pl.BlockSpec((tm, SSM_WIDTH), lambda i: (i, 0)),
                  pl.BlockSpec((tm, D_MODEL), lambda i: (i, 0)),
                  full(SSM_WIDTH, SSM_WIDTH), full(1, SSM_WIDTH),
                  full(D_MODEL, D_MODEL), full(1, D_MODEL), full(1, D_MODEL)],
        out_specs=[pl.BlockSpec((tm, D_MODEL), lambda i: (i, 0)),
                   pl.BlockSpec((D_MODEL, tm), lambda i: (0, i))],
        out_shape=[jax.ShapeDtypeStruct((t, D_MODEL), F32),
                   jax.ShapeDtypeStruct((D_MODEL, t), BF16)],
        compiler_params=_cparams(("parallel",)),
        name="merge_out_ln1",
    )(attn_o, y_ssm, x, glu_w, glu_b, w_out, ln_g, ln_b)


def _top_desc(vals, n):
    rows = []
    for r in range(n):
        mx = jnp.max(vals, axis=0, keepdims=True)
        rows.append(mx)
        if r + 1 < n:
            vals = jnp.where(vals == mx, -jnp.inf, vals)
    return rows


def _route_kernel(xt_ref, wq_ref, keys_ref, e1_ref, e2_ref, te_ref):
    k = PEER_TOPK
    qt = jnp.dot(wq_ref[...], xt_ref[...], preferred_element_type=F32).astype(BF16)
    for h in range(PEER_HEADS):
        tops = []
        for m in range(2):
            hm = 2 * h + m
            sc = jnp.dot(keys_ref[hm], qt[hm * PEER_NKEYS:(hm + 1) * PEER_NKEYS, :],
                         preferred_element_type=F32)
            tops.append((sc, _top_desc(sc, k + 1)))
        (sc1, a), (sc2, b) = tops
        ninf = jnp.full_like(a[0], -jnp.inf)
        pad = [ninf] * (-(k + 1) % 8)
        amat = jnp.concatenate(a + pad, axis=0)
        bmat = jnp.concatenate(b + pad, axis=0)
        row8 = lax.broadcasted_iota(jnp.int32, (8, amat.shape[1]), 0)
        groups = [a[0] + bmat]
        for r1 in range(2, 9):
            groups.append(jnp.where(row8 < (k + 1) // r1, a[r1 - 1] + bmat[0:8], -jnp.inf))
        groups.append(amat[8:] + b[0])
        cand = jnp.concatenate(groups, axis=0)
        best = _top_desc(cand, k + 1)
        zsum = jnp.ones_like(best[0])
        for r in range(1, k):
            zsum = zsum + jnp.exp(best[r] - best[0])
        theta = 0.5 * (best[k - 1] + best[k])
        e1_ref[h] = jnp.exp(sc1 - a[0]) / zsum
        e2_ref[h] = jnp.exp(sc2 - b[0])
        te_ref[h] = jnp.exp((theta - b[0]) - sc1)


def _route(x1t, wq_t, keys, tm):
    t = x1t.shape[1]
    big = pl.BlockSpec((PEER_HEADS, PEER_NKEYS, tm), lambda i: (0, 0, i))
    sds = jax.ShapeDtypeStruct((PEER_HEADS, PEER_NKEYS, t), F32)
    return pl.pallas_call(
        _route_kernel,
        grid=(t // tm,),
        in_specs=[pl.BlockSpec((D_MODEL, tm), lambda i: (0, i)),
                  pl.BlockSpec(wq_t.shape, lambda i: (0, 0)),
                  pl.BlockSpec(keys.shape, lambda i: (0, 0, 0))],
        out_specs=[big, big, big],
        out_shape=[sds, sds, sds],
        compiler_params=_cparams(("parallel",)),
        name="peer_route",
    )(x1t, wq_t, keys)


IPB = 4
UNIT_I = 2


def _peer_kernel(xt_ref, u_ref, vt_ref, e1_ref, e2_ref, te_ref, o_ref, *, tsub):
    eb = pl.program_id(1)

    @pl.when(eb == 0)
    def _zero():
        o_ref[...] = jnp.zeros(o_ref.shape, F32)

    n_sub = xt_ref.shape[1] // tsub
    upb = IPB // UNIT_I
    units = [(j, un) for j in range(n_sub) for un in range(upb)]
    urows = UNIT_I * PEER_NKEYS
    vrows = D_MODEL // upb

    def up(j, un):
        return jnp.dot(u_ref[un * urows:(un + 1) * urows, :],
                       xt_ref[:, j * tsub:(j + 1) * tsub], preferred_element_type=F32)

    def down_piece(j, a, piece):
        rows = slice(piece * vrows, (piece + 1) * vrows)
        o_ref[rows, j * tsub:(j + 1) * tsub] += jnp.dot(vt_ref[rows, :], a,
                                                        preferred_element_type=F32)

    def gate(j, ii, ht):
        i = eb * IPB + ii
        cols = slice(j * tsub, (j + 1) * tsub)
        e1rows = [e1_ref[h, pl.ds(i, 1), cols] for h in range(PEER_HEADS)]
        terows = [te_ref[h, pl.ds(i, 1), cols] for h in range(PEER_HEADS)]
        a_cols = []
        for c0 in range(0, tsub, LANES):
            hl = slice(c0, c0 + LANES)
            lc = slice(j * tsub + c0, j * tsub + c0 + LANES)
            w = None
            for h in range(PEER_HEADS):
                e2 = e2_ref[h, :, lc]
                wh = jnp.where(e2 >= terows[h][:, hl], e2, 0.0) * e1rows[h][:, hl]
                w = wh if w is None else w + wh
            a_cols.append((w * jax.nn.gelu(ht[:, hl])).astype(BF16))
        return a_cols[0] if len(a_cols) == 1 else jnp.concatenate(a_cols, axis=1)

    ht_next = up(*units[0])
    a_prev, a_rows = None, []
    for n, (j, un) in enumerate(units):
        ht = ht_next
        if n + 1 < len(units):
            ht_next = up(*units[n + 1])
        if a_prev is not None:
            down_piece(j - 1, a_prev, un)
        for k in range(UNIT_I):
            a_rows.append(gate(j, un * UNIT_I + k, ht[k * PEER_NKEYS:(k + 1) * PEER_NKEYS, :]))
        if un == upb - 1:
            a_prev, a_rows = jnp.concatenate(a_rows, axis=0), []
    for piece in range(upb):
        down_piece(n_sub - 1, a_prev, piece)


def _peer(x1t, u_tab, vt_tab, e1, e2, te_thr, tm, tsub):
    t = x1t.shape[1]
    te = IPB * PEER_NKEYS
    n_eb = u_tab.shape[0] // te
    once = pl.Buffered(1)
    big = pl.BlockSpec((PEER_HEADS, PEER_NKEYS, tm), lambda i, e: (0, 0, i), pipeline_mode=once)
    return pl.pallas_call(
        functools.partial(_peer_kernel, tsub=tsub),
        grid=(t // tm, n_eb),
        in_specs=[pl.BlockSpec((D_MODEL, tm), lambda i, e: (0, i), pipeline_mode=once),
                  pl.BlockSpec((te, D_MODEL), lambda i, e: (e, 0)),
                  pl.BlockSpec((D_MODEL, te), lambda i, e: (0, e)),
                  big, big, big],
        out_specs=pl.BlockSpec((D_MODEL, tm), lambda i, e: (0, i)),
        out_shape=jax.ShapeDtypeStruct((D_MODEL, t), F32),
        compiler_params=_cparams(("parallel", "arbitrary")),
        name="peer_experts",
    )(x1t, u_tab, vt_tab, e1, e2, te_thr)


def _ln2_kernel(x1_ref, ft_ref, g_ref, b_ref, o_ref):
    o_ref[...] = _layer_norm(DEEPNORM_ALPHA * x1_ref[...] + ft_ref[...].T,
                             g_ref[...], b_ref[...])


def _ln2(x1, ffn_t, ln_g, ln_b, tm):
    t = x1.shape[0]
    return pl.pallas_call(
        _ln2_kernel,
        grid=(t // tm,),
        in_specs=[pl.BlockSpec((tm, D_MODEL), lambda i: (i, 0)),
                  pl.BlockSpec((D_MODEL, tm), lambda i: (0, i)),
                  pl.BlockSpec((1, D_MODEL), lambda i: (0, 0)),
                  pl.BlockSpec((1, D_MODEL), lambda i: (0, 0))],
        out_specs=pl.BlockSpec((tm, D_MODEL), lambda i: (i, 0)),
        out_shape=jax.ShapeDtypeStruct((t, D_MODEL), F32),
        compiler_params=_cparams(("parallel",)),
        name="ln2",
    )(x1, ffn_t, ln_g, ln_b)


def _prompt_buckets(tb):
    r = np.arange(tb)[:, None]
    c = np.arange(tb)[None, :]
    diag = np.where(c <= r, _bucket_np(r - c), -1)
    sub = _bucket_np(r - c + tb)
    return np.stack([diag, sub]).astype(np.int32)


def _sample_buckets(past_len, dec_seq):
    qpos = past_len + np.tile(np.arange(dec_seq), 2)[:, None]
    w = PPS * PAGE_SIZE
    kpos = (past_len - w) + np.arange(w)[None, :]
    last = _bucket_np(qpos - kpos)
    jn = np.arange(NEWK)[None, :]
    knew = past_len + jn
    new = np.where((jn < dec_seq) & (knew <= qpos), _bucket_np(qpos - knew), -1)
    return last[None].astype(np.int32), new[None].astype(np.int32)


def _group_tail(x1, x1t, w_query_t, keys, u_tab, vt_tab, ln_g, ln_b, tm_route, tm_peer, tm_ln):
    e1, e2, te_thr = _route(x1t, w_query_t, keys, tm_route)
    ffn_t = _peer(x1t, u_tab, vt_tab, e1, e2, te_thr, tm_peer, min(tm_peer, 256))
    return _ln2(x1, ffn_t, ln_g, ln_b, tm_ln)


def kernel(x_prompt, x_sample, cache_k, cache_v, state_ssm_re, state_ssm_im, page_table, w_in, lambda_q1, lambda_k1, lambda_q2, lambda_k2, attn_subln_g, rel_bias, ssm_a_re, ssm_a_im, ssm_b_re, ssm_b_im, ssm_c_re, ssm_c_im, ssm_d, ssm_log_dt, ssm_glu_w, ssm_glu_b, w_out, ln1_g, ln1_b, peer_w_query, peer_sub_keys, peer_u, peer_v, ln2_g, ln2_b):
    l = 0
    seq = x_prompt.shape[1]
    dec_b, dec_s = x_sample.shape[0], x_sample.shape[1]
    n_pages = page_table.shape[1]
    past_len = n_pages * PAGE_SIZE
    g, p, c = N_SSM_GROUPS, SSM_STATE, SSM_GROUP
    lam_init = 0.8 - 0.6 * math.exp(-0.3 * l)
    lam = (jnp.exp(jnp.sum(lambda_q1[l] * lambda_k1[l]))
           - jnp.exp(jnp.sum(lambda_q2[l] * lambda_k2[l])) + lam_init).reshape(1).astype(F32)

    w_in_b = w_in[l].astype(BF16)
    glu_w_b = ssm_glu_w[l].astype(BF16)
    w_out_b = w_out[l].astype(BF16)
    wq_t = peer_w_query[l].T.astype(BF16)
    keys = peer_sub_keys[l].reshape(2 * PEER_HEADS, PEER_NKEYS, PEER_NKEYS).astype(BF16)
    u_tab = peer_u[l].astype(BF16)
    vt_tab = peer_v[l].T.astype(BF16)
    subln_g = attn_subln_g[l].reshape(1, HEAD_DIM)
    glu_b = ssm_glu_b[l].reshape(1, SSM_WIDTH)
    g1, b1 = ln1_g[l].reshape(1, D_MODEL), ln1_b[l].reshape(1, D_MODEL)
    g2, b2 = ln2_g[l].reshape(1, D_MODEL), ln2_b[l].reshape(1, D_MODEL)
    coefs = _ssm_coefs(ssm_log_dt[l], ssm_a_re[l], ssm_a_im[l],
                       ssm_b_re[l].transpose(0, 2, 1), ssm_b_im[l].transpose(0, 2, 1),
                       ssm_c_re[l], ssm_c_im[l])

    xp = x_prompt.reshape(seq, D_MODEL)
    tb = 1024
    qkvu = _proj(xp, w_in_b, 512, ATTN_WIDTH)
    bias_p = _bias_tiles(rel_bias, _prompt_buckets(tb), LOG2E)
    attn_p = _attn_prompt(qkvu, bias_p, lam, subln_g, tb, lam_init)
    y_p, hr_p, hi_p = _ssm_prompt(qkvu[3], coefs, ssm_d[l])
    x1_p, x1t_p = _merge(attn_p, y_p, xp, glu_w_b, glu_b, w_out_b, g1, b1, 256)
    out_p = _group_tail(x1_p, x1t_p, wq_t, keys, u_tab, vt_tab, g2, b2, 256, 1024, 256)

    ts = dec_b * dec_s
    xs = x_sample.reshape(ts, D_MODEL)
    qkvu_s = _proj(xs, w_in_b, ts, ATTN_WIDTH)
    q_s = qkvu_s[0].reshape(dec_b, dec_s, N_HEADS, HEAD_DIM) * (HALF_DIM ** -0.5)
    lane = np.arange(HEAD_DIM)
    halves = jnp.asarray(np.stack([lane < HALF_DIM, lane >= HALF_DIM]).astype(np.float32))
    q2 = (q_s[:, None] * halves[None, :, None, None, :])
    q2 = q2.transpose(0, 3, 1, 2, 4).reshape(dec_b, N_HEADS, QROWS, HEAD_DIM)
    pad_new = lambda a: jnp.pad(
        a.reshape(dec_b, dec_s, N_HEADS, HEAD_DIM).transpose(0, 2, 1, 3),
        ((0, 0), (0, 0), (0, NEWK - dec_s), (0, 0)))
    bl, bn = _sample_buckets(past_len, dec_s)
    bias_last = _bias_tiles(rel_bias, bl)[:, 0]
    bias_new = _bias_tiles(rel_bias, bn)[:, 0]
    attn_s = _attn_sample(q2, pad_new(qkvu_s[1]), pad_new(qkvu_s[2]), cache_k[l], cache_v[l],
                          page_table, bias_last, bias_new, lam, subln_g, lam_init)
    u_s = qkvu_s[3].reshape(dec_b, dec_s, g, c).transpose(2, 1, 0, 3).reshape(g, ts, c)
    y_g, hr_s, hi_s = _ssm_sample(u_s, state_ssm_re[l].transpose(1, 0, 2),
                                  state_ssm_im[l].transpose(1, 0, 2), coefs,
                                  ssm_c_re[l], ssm_c_im[l], ssm_d[l], dec_b, dec_s)
    y_s = y_g.reshape(g, dec_s, dec_b, c).transpose(2, 1, 0, 3).reshape(ts, SSM_WIDTH)
    x1_s, x1t_s = _merge(attn_s.reshape(ts, ATTN_WIDTH), y_s, xs, glu_w_b, glu_b, w_out_b,
                         g1, b1, ts)
    out_s = _group_tail(x1_s, x1t_s, wq_t, keys, u_tab, vt_tab, g2, b2, ts, ts, ts)

    kv = lambda a, b_, s_: a.reshape(1, b_, s_, N_HEADS, HEAD_DIM)
    return (out_p.reshape(1, seq, D_MODEL), out_s.reshape(dec_b, dec_s, D_MODEL),
            kv(qkvu[1], 1, seq), kv(qkvu[2], 1, seq),
            hr_p.reshape(1, 1, g, p), hi_p.reshape(1, 1, g, p),
            kv(qkvu_s[1], dec_b, dec_s), kv(qkvu_s[2], dec_b, dec_s),
            hr_s.transpose(1, 0, 2)[None], hi_s.transpose(1, 0, 2)[None])
```

```python
import functools
import math

import numpy as np
import jax
import jax.numpy as jnp
from jax import lax
from jax.experimental import pallas as pl
from jax.experimental.pallas import tpu as pltpu

F32 = jnp.float32
BF16 = jnp.bfloat16

D_MODEL = 2048
PAGE_SIZE = 128
ATTN_WIDTH = 1024
SSM_WIDTH = 1024
N_HEADS = 8
HEAD_DIM = 128
HALF_DIM = 64
N_BUCKETS = 32
MAX_DISTANCE = 128
SSM_GROUP = 16
N_SSM_GROUPS = 64
SSM_STATE = 64
PEER_HEADS = 8
PEER_NKEYS = 128
PEER_TOPK = 16
LN_EPS = 1e-5
DEPTH = 1
DEEPNORM_ALPHA = (2.0 * DEPTH) ** 0.25

LANES = 128
SCAN_CHUNK = 8
GBLK = LANES // SSM_GROUP
NEG = -1e30
LOG2E = 1.4426950408889634
VMEM_LIMIT = 56 * 1024 * 1024

_HI = lax.Precision.HIGHEST


def _cparams(sem):
    return pltpu.CompilerParams(dimension_semantics=sem, vmem_limit_bytes=VMEM_LIMIT)


def _rep(x, n):
    return x if n == 1 else jnp.concatenate([x] * n, axis=1)


def _proj_kernel(x_ref, w_ref, *o_refs):
    xb = x_ref[...].astype(BF16)
    tn = o_refs[0].shape[1]
    for j, o_ref in enumerate(o_refs):
        o_ref[...] = jnp.dot(xb, w_ref[:, j * tn:(j + 1) * tn], preferred_element_type=F32)


def _proj(x, w, tm, tn):
    t, k = x.shape
    n = w.shape[1]
    return pl.pallas_call(
        _proj_kernel,
        grid=(t // tm,),
        in_specs=[pl.BlockSpec((tm, k), lambda i: (i, 0)),
                  pl.BlockSpec((k, n), lambda i: (0, 0), pipeline_mode=pl.Buffered(1))],
        out_specs=[pl.BlockSpec((tm, tn), lambda i: (i, 0))] * (n // tn),
        out_shape=[jax.ShapeDtypeStruct((t, tn), F32)] * (n // tn),
        compiler_params=_cparams(("parallel",)),
        name="in_proj",
    )(x, w)


def _bucket_np(d):
    n = np.maximum(d, 0)
    max_exact = N_BUCKETS // 2
    nf = np.maximum(n, 1).astype(np.float64)
    large = max_exact + (np.log(nf / max_exact) / math.log(MAX_DISTANCE / max_exact)
                         * (N_BUCKETS - max_exact)).astype(np.int32)
    large = np.minimum(large, N_BUCKETS - 1)
    return np.where(n < max_exact, n, large).astype(np.int32)


def _bias_kernel(rb_ref, bkt_ref, o_ref, *, scale):
    h = pl.program_id(0)
    bkt = bkt_ref[0]
    far = rb_ref[N_BUCKETS - 1, h]
    acc = jnp.where(bkt < 0, NEG, 0.0).astype(F32)
    for b in range(N_BUCKETS - 1):
        acc = jnp.where(bkt == b, (rb_ref[b, h] - far) * scale, acc)
    o_ref[0, 0] = acc


def _bias_tiles(rel_bias, bkt, scale=1.0):
    n, r, c = bkt.shape
    return pl.pallas_call(
        functools.partial(_bias_kernel, scale=scale),
        grid=(N_HEADS, n),
        in_specs=[pl.BlockSpec(memory_space=pltpu.SMEM),
                  pl.BlockSpec((1, r, c), lambda h, i: (i, 0, 0))],
        out_specs=pl.BlockSpec((1, 1, r, c), lambda h, i: (h, i, 0, 0)),
        out_shape=jax.ShapeDtypeStruct((N_HEADS, n, r, c), F32),
        compiler_params=_cparams(("parallel", "arbitrary")),
        name="bias_tiles",
    )(rel_bias, jnp.asarray(bkt))


def _band_bias_kernel(rb_ref, bkt_ref, o_ref, *, tb, scale):
    h = pl.program_id(0)
    far = rb_ref[N_BUCKETS - 1, h]

    def block(bkt):
        acc = jnp.where(bkt < 0, NEG, 0.0).astype(F32)
        for b in range(N_BUCKETS - 1):
            acc = jnp.where(bkt == b, (rb_ref[b, h] - far) * scale, acc)
        return acc

    on_diag, below = block(bkt_ref[0]), block(bkt_ref[1])
    zero = jnp.zeros((LANES, LANES), F32)
    masked = jnp.full((LANES, LANES), NEG, F32)
    nblk = tb // LANES
    for a in range(nblk):
        row = [on_diag if b == a else below if b == a - 1 else zero if b < a else masked
               for b in range(nblk)]
        o_ref[0, 0, a * LANES:(a + 1) * LANES, :] = jnp.concatenate(row, axis=1)
    o_ref[0, 1] = jnp.zeros((tb, tb), F32)
    o_ref[0, 1, 0:LANES, tb - LANES:tb] = below


def _band_bias(rel_bias, tb, scale):
    assert MAX_DISTANCE <= LANES and tb % LANES == 0
    r = np.arange(LANES)[:, None]
    c = np.arange(LANES)[None, :]
    bkt = np.stack([np.where(c <= r, _bucket_np(r - c), -1), _bucket_np(r - c + LANES)])
    return pl.pallas_call(
        functools.partial(_band_bias_kernel, tb=tb, scale=scale),
        grid=(N_HEADS,),
        in_specs=[pl.BlockSpec(memory_space=pltpu.SMEM),
                  pl.BlockSpec((2, LANES, LANES), lambda h: (0, 0, 0))],
        out_specs=pl.BlockSpec((1, 2, tb, tb), lambda h: (h, 0, 0, 0)),
        out_shape=jax.ShapeDtypeStruct((N_HEADS, 2, tb, tb), F32),
        compiler_params=_cparams(("parallel",)),
        name="band_bias",
    )(rel_bias, jnp.asarray(bkt.astype(np.int32)))


def _subln(attn, g_row, lam_init):
    ms = jnp.mean(jnp.square(attn), axis=-1, keepdims=True)
    return attn * lax.rsqrt(ms + LN_EPS) * g_row * (1.0 - lam_init)


def _attn_kernel(qi_ref, ki_ref, lam_ref, q_ref, k_ref, v_ref, bias_ref, g_ref, o_ref,
                 q2_sc, m_sc, l_sc, acc_sc, *, tb, lam_init):
    s = pl.program_id(1)
    qi = qi_ref[s]
    ki = ki_ref[s]

    @pl.when(ki == 0)
    def _init():
        q = q_ref[...] * (HALF_DIM ** -0.5 * LOG2E)
        lane = lax.broadcasted_iota(jnp.int32, q.shape, 1)
        q2_sc[0:tb, :] = jnp.where(lane < HALF_DIM, q, 0.0).astype(BF16)
        q2_sc[tb:2 * tb, :] = jnp.where(lane >= HALF_DIM, q, 0.0).astype(BF16)
        m_sc[...] = jnp.full(m_sc.shape, NEG, F32)
        l_sc[...] = jnp.zeros(l_sc.shape, F32)
        acc_sc[...] = jnp.zeros(acc_sc.shape, F32)

    nt = (((1,), (1,)), ((), ()))
    rc = min(tb, ATTN_ROW_CHUNK)

    def update(bias_tile, causal):
        kb = k_ref[...].astype(BF16)
        vb = v_ref[...].astype(BF16)
        n_chunks = 2 * tb // rc

        def width(c):
            return (c * rc) % tb + rc if causal else tb

        def qk(c):
            w = width(c)
            sc = lax.dot_general(q2_sc[c * rc:(c + 1) * rc, :], kb[0:w], nt,
                                 preferred_element_type=F32)
            if bias_tile is not None:
                r0 = (c * rc) % tb
                sc = sc + bias_ref[0, bias_tile, r0:r0 + rc, 0:w]
            return sc

        def softmax(c, sc):
            rows = slice(c * rc, (c + 1) * rc)
            m_prev = m_sc[rows]
            m_new = jnp.maximum(m_prev, jnp.max(sc, axis=-1, keepdims=True))
            alpha = jnp.exp2(m_prev - m_new)
            p = jnp.exp2(sc - _rep(m_new, sc.shape[1] // LANES))
            l_sc[rows] = alpha * l_sc[rows] + jnp.sum(p, axis=-1, keepdims=True)
            m_sc[rows] = m_new
            return p.astype(BF16), alpha

        def pv(c, p, alpha):
            rows = slice(c * rc, (c + 1) * rc)
            acc_sc[rows] = alpha * acc_sc[rows] + jnp.dot(p, vb[0:width(c)],
                                                          preferred_element_type=F32)

        sc_next = qk(0)
        for c in range(n_chunks):
            sc = sc_next
            if c + 1 < n_chunks:
                sc_next = qk(c + 1)
            p, alpha = softmax(c, sc)
            pv(c, p, alpha)

    @pl.when(ki < qi - 1)
    def _far():
        update(None, False)

    @pl.when(ki == qi - 1)
    def _sub():
        update(1, False)

    @pl.when(ki == qi)
    def _diag():
        update(0, True)
        o = acc_sc[...] / l_sc[...]
        attn = o[0:tb] - lam_ref[0] * o[tb:2 * tb]
        o_ref[...] = _subln(attn, g_ref[...], lam_init)


def _attn_prompt(q, k, v, bias, lam, subln_g, tb, lam_init):
    t = q.shape[0]
    nb = t // tb
    qi_idx = np.concatenate([np.full(i + 1, i) for i in range(nb)]).astype(np.int32)
    ki_idx = np.concatenate([np.arange(i + 1) for i in range(nb)]).astype(np.int32)
    grid_spec = pltpu.PrefetchScalarGridSpec(
        num_scalar_prefetch=2,
        grid=(N_HEADS, len(qi_idx)),
        in_specs=[
            pl.BlockSpec(memory_space=pltpu.SMEM),
            pl.BlockSpec((tb, HEAD_DIM), lambda h, s, qi, ki: (qi[s], h)),
            pl.BlockSpec((tb, HEAD_DIM), lambda h, s, qi, ki: (ki[s], h)),
            pl.BlockSpec((tb, HEAD_DIM), lambda h, s, qi, ki: (ki[s], h)),
            pl.BlockSpec((1, 2, tb, tb), lambda h, s, qi, ki: (h, 0, 0, 0)),
            pl.BlockSpec((1, HEAD_DIM), lambda h, s, qi, ki: (0, 0)),
        ],
        out_specs=pl.BlockSpec((tb, HEAD_DIM), lambda h, s, qi, ki: (qi[s], h)),
        scratch_shapes=[
            pltpu.VMEM((2 * tb, HEAD_DIM), BF16),
            pltpu.VMEM((2 * tb, LANES), F32),
            pltpu.VMEM((2 * tb, LANES), F32),
            pltpu.VMEM((2 * tb, HEAD_DIM), F32),
        ],
    )
    return pl.pallas_call(
        functools.partial(_attn_kernel, tb=tb, lam_init=lam_init),
        grid_spec=grid_spec,
        out_shape=jax.ShapeDtypeStruct((t, ATTN_WIDTH), F32),
        compiler_params=_cparams(("parallel", "arbitrary")),
        name="attn_prompt",
    )(jnp.asarray(qi_idx), jnp.asarray(ki_idx), lam, q, k, v, bias, subln_g)


PPS = 8
QROWS = 8
NEWK = 16


def _decode_kernel(pt_ref, lam_ref, q2_ref, kn_ref, vn_ref, bl_ref, bn_ref, g_ref, *rest,
                   n_steps, lam_init):
    k_refs = rest[0:PPS]
    v_refs = rest[PPS:2 * PPS]
    o_ref = rest[2 * PPS]
    m_sc, l_sc, acc_sc = rest[2 * PPS + 1:]
    p_idx = pl.program_id(1)
    last = p_idx == n_steps - 1

    @pl.when(p_idx == 0)
    def _init():
        m_sc[...] = jnp.full(m_sc.shape, NEG, F32)
        l_sc[...] = jnp.zeros(l_sc.shape, F32)
        acc_sc[...] = jnp.zeros(acc_sc.shape, F32)

    last_f = jnp.where(last, 1.0, 0.0).astype(F32)

    def softmax_step(h, sc):
        m_prev = m_sc[h]
        m_new = jnp.maximum(m_prev, jnp.max(sc, axis=-1, keepdims=True))
        alpha = jnp.exp(m_prev - m_new)
        p = jnp.exp(sc - _rep(m_new, sc.shape[1] // LANES) if sc.shape[1] >= LANES
                    else sc - m_new[:, 0:sc.shape[1]])
        l_sc[h] = alpha * l_sc[h] + jnp.sum(p, axis=-1, keepdims=True)
        m_sc[h] = m_new
        return p.astype(BF16), alpha

    def accumulate(h, p, alpha, vmat):
        acc_sc[h] = alpha * acc_sc[h] + jnp.dot(p, vmat, preferred_element_type=F32)

    def update(h, sc, vmat):
        p, alpha = softmax_step(h, sc)
        accumulate(h, p, alpha, vmat)

    def head_rows(refs, h):
        rows_h = pl.ds(h, PAGE_SIZE, stride=N_HEADS)
        return jnp.concatenate([r[0, rows_h, :] for r in refs], axis=0).astype(BF16)

    scores = []
    for h in range(N_HEADS):
        q2 = q2_ref[0, h].astype(BF16)
        sc = lax.dot_general(q2, head_rows(k_refs, h), (((1,), (1,)), ((), ())),
                             preferred_element_type=F32)
        scores.append(sc + bl_ref[h] * last_f)
    probs = [softmax_step(h, scores[h]) for h in range(N_HEADS)]
    for h in range(N_HEADS):
        accumulate(h, probs[h][0], probs[h][1], head_rows(v_refs, h))

    @pl.when(last)
    def _finish():
        for h in range(N_HEADS):
            q2 = q2_ref[0, h].astype(BF16)
            kn = kn_ref[0, h].astype(BF16)
            vn = vn_ref[0, h].astype(BF16)
            sc = lax.dot_general(q2, kn, (((1,), (1,)), ((), ())), preferred_element_type=F32)
            update(h, sc + bn_ref[h], vn)
            o = acc_sc[h] / l_sc[h]
            half = QROWS // 2
            attn = o[0:half] - lam_ref[0] * o[half:QROWS]
            o_ref[0, :, h * HEAD_DIM:(h + 1) * HEAD_DIM] = _subln(attn, g_ref[...], lam_init)


def _attn_sample(q2, k_new, v_new, cache_k, cache_v, page_table, bias_last, bias_new,
                 lam, subln_g, lam_init):
    nb, n_pages = page_table.shape
    n_steps = n_pages // PPS
    dec_seq = QROWS // 2
    pages = lambda cache: cache.reshape(cache.shape[0], PAGE_SIZE * N_HEADS, HEAD_DIM)

    def page_spec(j):
        return pl.BlockSpec(
            (1, PAGE_SIZE * N_HEADS, HEAD_DIM),
            lambda b, p, pt, j=j: (pt[b * n_pages + p * PPS + j], 0, 0))

    grid_spec = pltpu.PrefetchScalarGridSpec(
        num_scalar_prefetch=1,
        grid=(nb, n_steps),
        in_specs=[
            pl.BlockSpec(memory_space=pltpu.SMEM),
            pl.BlockSpec((1, N_HEADS, QROWS, HEAD_DIM), lambda b, p, pt: (b, 0, 0, 0)),
            pl.BlockSpec((1, N_HEADS, NEWK, HEAD_DIM), lambda b, p, pt: (b, 0, 0, 0)),
            pl.BlockSpec((1, N_HEADS, NEWK, HEAD_DIM), lambda b, p, pt: (b, 0, 0, 0)),
            pl.BlockSpec((N_HEADS, QROWS, PPS * PAGE_SIZE), lambda b, p, pt: (0, 0, 0)),
            pl.BlockSpec((N_HEADS, QROWS, NEWK), lambda b, p, pt: (0, 0, 0)),
            pl.BlockSpec((1, HEAD_DIM), lambda b, p, pt: (0, 0)),
        ] + [page_spec(j) for j in range(PPS)] + [page_spec(j) for j in range(PPS)],
        out_specs=pl.BlockSpec((1, dec_seq, ATTN_WIDTH), lambda b, p, pt: (b, 0, 0)),
        scratch_shapes=[
            pltpu.VMEM((N_HEADS, QROWS, LANES), F32),
            pltpu.VMEM((N_HEADS, QROWS, LANES), F32),
            pltpu.VMEM((N_HEADS, QROWS, HEAD_DIM), F32),
        ],
    )
    return pl.pallas_call(
        functools.partial(_decode_kernel, n_steps=n_steps, lam_init=lam_init),
        grid_spec=grid_spec,
        out_shape=jax.ShapeDtypeStruct((nb, dec_seq, ATTN_WIDTH), F32),
        compiler_params=_cparams(("parallel", "arbitrary")),
        name="attn_sample",
    )(page_table.reshape(-1), lam, q2, k_new, v_new, bias_last, bias_new, subln_g,
      *([pages(cache_k)] * PPS), *([pages(cache_v)] * PPS))


def _ssm_coef_kernel(ldt_ref, ar_ref, ai_ref, bt_re_ref, bt_im_ref, c_re_ref, c_im_ref,
                     t_ref, s_ref, m_ref,
                     abar_re_ref, abar_im_ref, apow_re_ref, apow_im_ref,
                     bb_re_ref, bb_im_ref):
    lc, c, p = SCAN_CHUNK, SSM_GROUP, SSM_STATE
    hw = GBLK * p
    dt = jnp.exp(ldt_ref[0])
    ar, ai = ar_ref[0], ai_ref[0]
    mag = jnp.exp(ar * dt)
    abar_re = mag * jnp.cos(ai * dt)
    abar_im = mag * jnp.sin(ai * dt)
    nr, ni = abar_re - 1.0, abar_im
    den = ar * ar + ai * ai
    f_re = ((nr * ar + ni * ai) / den)[:, None, :]
    f_im = ((ni * ar - nr * ai) / den)[:, None, :]
    bt_re, bt_im = bt_re_ref[0], bt_im_ref[0]
    bb_re = f_re * bt_re - f_im * bt_im
    bb_im = f_re * bt_im + f_im * bt_re
    c_re, c_im = c_re_ref[0], c_im_ref[0]

    pw = [(jnp.ones_like(abar_re), jnp.zeros_like(abar_re))]
    for _ in range(lc):
        pr, pi = pw[-1]
        pw.append((pr * abar_re - pi * abar_im, pr * abar_im + pi * abar_re))

    rid = lax.broadcasted_iota(jnp.int32, (LANES, hw), 0) // c
    cid = lax.broadcasted_iota(jnp.int32, (LANES, hw), 1) // p
    same_group = rid == cid

    def bdiag(x):
        flat = x.reshape(LANES, p)
        return jnp.where(same_group, jnp.concatenate([flat] * GBLK, axis=1), 0.0)

    def cmul(xr, xi, w):
        wr, wi = w[0][:, None, :], w[1][:, None, :]
        return xr * wr - xi * wi, xr * wi + xi * wr

    l_re, l_im = bdiag(bb_re), bdiag(bb_im)
    nt = (((1,), (1,)), ((), ()))
    zero_tile = jnp.zeros((LANES, LANES), BF16)
    bd = []
    for tau in range(lc + 1):
        cr, ci = cmul(c_re, c_im, pw[tau])
        r_re, r_im = bdiag(cr), bdiag(ci)
        if tau < lc:
            bd.append((lax.dot_general(l_re, r_re, nt, precision=_HI, preferred_element_type=F32)
                       - lax.dot_general(l_im, r_im, nt, precision=_HI,
                                         preferred_element_type=F32)).astype(BF16))
        if tau > 0:
            cols = slice((tau - 1) * LANES, tau * LANES)
            m_ref[0, 0:hw, cols] = r_re.T.astype(BF16)
            m_ref[0, hw:2 * hw, cols] = (-r_im).T.astype(BF16)
    for s in range(lc):
        rows = slice(s * LANES, (s + 1) * LANES)
        for r in range(lc):
            t_ref[0, rows, r * LANES:(r + 1) * LANES] = bd[r - s] if r >= s else zero_tile
        wr, wi = cmul(bb_re, bb_im, pw[lc - 1 - s])
        s_ref[0, rows, 0:hw] = bdiag(wr).astype(BF16)
        s_ref[0, rows, hw:2 * hw] = bdiag(wi).astype(BF16)
    abar_re_ref[0] = abar_re
    abar_im_ref[0] = abar_im
    apow_re_ref[0] = pw[lc][0]
    apow_im_ref[0] = pw[lc][1]
    bb_re_ref[0] = bb_re
    bb_im_ref[0] = bb_im


def _ssm_coefs(log_dt, a_re, a_im, bt_re, bt_im, c_re, c_im):
    g, p, c, lc = N_SSM_GROUPS, SSM_STATE, SSM_GROUP, SCAN_CHUNK
    nb, hw, w = g // GBLK, GBLK * p, lc * LANES
    vec = pl.BlockSpec((1, GBLK, p), lambda i: (i, 0, 0))
    mat = pl.BlockSpec((1, GBLK, c, p), lambda i: (i, 0, 0, 0))
    sds = jax.ShapeDtypeStruct
    blk = lambda a: a.reshape((nb, GBLK) + a.shape[1:])
    return pl.pallas_call(
        _ssm_coef_kernel,
        grid=(nb,),
        in_specs=[pl.BlockSpec((1, GBLK, 1), lambda i: (i, 0, 0)), vec, vec, mat, mat, mat, mat],
        out_specs=[pl.BlockSpec((1, w, w), lambda i: (i, 0, 0)),
                   pl.BlockSpec((1, w, 2 * hw), lambda i: (i, 0, 0)),
                   pl.BlockSpec((1, 2 * hw, w), lambda i: (i, 0, 0)),
                   vec, vec, vec, vec, mat, mat],
        out_shape=[sds((nb, w, w), BF16), sds((nb, w, 2 * hw), BF16), sds((nb, 2 * hw, w), BF16)]
        + [sds((nb, GBLK, p), F32)] * 4 + [sds((nb, GBLK, c, p), F32)] * 2,
        compiler_params=_cparams(("parallel",)),
        name="ssm_coefs",
    )(log_dt.reshape(nb, GBLK, 1), blk(a_re), blk(a_im), blk(bt_re), blk(bt_im),
      blk(c_re), blk(c_im))


def _ssm_local_kernel(u_ref, s_ref, o_ref):
    o_ref[0] = jnp.dot(u_ref[0].astype(BF16), s_ref[0], preferred_element_type=F32)


def _ssm_local(u2, smat):
    gp, nk, w = u2.shape
    n = smat.shape[2]
    return pl.pallas_call(
        _ssm_local_kernel,
        grid=(gp,),
        in_specs=[pl.BlockSpec((1, nk, w), lambda i: (i, 0, 0)),
                  pl.BlockSpec((1, w, n), lambda i: (i, 0, 0))],
        out_specs=pl.BlockSpec((1, nk, n), lambda i: (i, 0, 0)),
        out_shape=jax.ShapeDtypeStruct((gp, nk, n), F32),
        compiler_params=_cparams(("parallel",)),
        name="ssm_local",
    )(u2, smat)


def _ssm_rec_kernel(s_ref, are_ref, aim_ref, h_ref, fin_ref, st_sc):
    kc = s_ref.shape[0]
    w = are_ref.shape[1]
    a_re = are_ref[...]
    a_im = aim_ref[...]

    @pl.when(pl.program_id(0) == 0)
    def _init():
        st_sc[...] = jnp.zeros(st_sc.shape, F32)

    def body(k, carry):
        hr, hi = carry
        h_ref[k] = jnp.concatenate([hr, hi], axis=1)
        sk = s_ref[k]
        return (a_re * hr - a_im * hi + sk[:, 0:w], a_re * hi + a_im * hr + sk[:, w:2 * w])

    hr, hi = lax.fori_loop(0, kc, body, (st_sc[0], st_sc[1]))
    st_sc[0] = hr
    st_sc[1] = hi
    fin_ref[...] = jnp.concatenate([hr, hi], axis=1)


def _ssm_rec(s_kg, apow_re, apow_im):
    nk, gp, w2 = s_kg.shape
    kc = min(nk, 256)
    return pl.pallas_call(
        _ssm_rec_kernel,
        grid=(nk // kc,),
        in_specs=[pl.BlockSpec((kc, gp, w2), lambda i: (i, 0, 0)),
                  pl.BlockSpec((gp, w2 // 2), lambda i: (0, 0)),
                  pl.BlockSpec((gp, w2 // 2), lambda i: (0, 0))],
        out_specs=[pl.BlockSpec((kc, gp, w2), lambda i: (i, 0, 0)),
                   pl.BlockSpec((gp, w2), lambda i: (0, 0))],
        out_shape=[jax.ShapeDtypeStruct((nk, gp, w2), F32),
                   jax.ShapeDtypeStruct((gp, w2), F32)],
        scratch_shapes=[pltpu.VMEM((2, gp, w2 // 2), F32)],
        compiler_params=_cparams(("arbitrary",)),
        name="ssm_rec",
    )(s_kg, apow_re, apow_im)


def _ssm_out_kernel(u_ref, uh_ref, t_ref, h_ref, m_ref, d_ref, o_ref):
    y = jnp.dot(u_ref[0].astype(BF16), t_ref[0], preferred_element_type=F32)
    y = y + jnp.dot(h_ref[0].astype(BF16), m_ref[0], preferred_element_type=F32)
    o_ref[0] = y + d_ref[0] * uh_ref[0]


def _ssm_out(u2, tmat, hst, mmat, dtile):
    nb, nk, w = u2.shape
    hw = hst.shape[2]
    nh = 2
    wn = w // nh
    return pl.pallas_call(
        _ssm_out_kernel,
        grid=(nb, nh),
        in_specs=[pl.BlockSpec((1, nk, w), lambda i, j: (i, 0, 0)),
                  pl.BlockSpec((1, nk, wn), lambda i, j: (i, 0, j)),
                  pl.BlockSpec((1, w, wn), lambda i, j: (i, 0, j)),
                  pl.BlockSpec((1, nk, hw), lambda i, j: (i, 0, 0)),
                  pl.BlockSpec((1, hw, wn), lambda i, j: (i, 0, j)),
                  pl.BlockSpec((1, 1, wn), lambda i, j: (i, 0, j))],
        out_specs=pl.BlockSpec((1, nk, wn), lambda i, j: (i, 0, j)),
        out_shape=jax.ShapeDtypeStruct((nb, nk, w), F32),
        compiler_params=_cparams(("parallel", "arbitrary")),
        name="ssm_out",
    )(u2, u2, tmat, hst, mmat, dtile)


def _ssm_prompt(u, coefs, ssm_d):
    tmat, smat, mmat, _, _, apow_re, apow_im, _, _ = coefs
    t = u.shape[0]
    g, p, lc = N_SSM_GROUPS, SSM_STATE, SCAN_CHUNK
    nk, nb = t // lc, g // GBLK
    hw = GBLK * p
    u2 = u.reshape(nk, lc, nb, LANES).transpose(2, 0, 1, 3).reshape(nb, nk, lc * LANES)
    s_loc = _ssm_local(u2, smat)
    hst, fin = _ssm_rec(s_loc.transpose(1, 0, 2), apow_re.reshape(nb, hw),
                        apow_im.reshape(nb, hw))
    dtile = jnp.tile(ssm_d.reshape(nb, 1, LANES), (1, 1, lc))
    y2 = _ssm_out(u2, tmat, hst.transpose(1, 0, 2), mmat, dtile)
    y = y2.reshape(nb, nk, lc, LANES).transpose(1, 2, 0, 3).reshape(t, SSM_WIDTH)
    return y, fin[:, :hw].reshape(g, p), fin[:, hw:].reshape(g, p)


def _ssm_sample_kernel(u_ref, hr_ref, hi_ref, are_ref, aim_ref, bbr_ref, bbi_ref,
                       cr_ref, ci_ref, d_ref, y_ref, fr_ref, fi_ref, *, nb, steps):
    u = u_ref[0]
    nt = (((1,), (0,)), ((), ()))
    bu_re = lax.dot_general(u, bbr_ref[0], nt, precision=_HI, preferred_element_type=F32)
    bu_im = lax.dot_general(u, bbi_ref[0], nt, precision=_HI, preferred_element_type=F32)
    a_re, a_im = are_ref[0], aim_ref[0]
    hr, hi = hr_ref[0], hi_ref[0]
    hs_re, hs_im = [], []
    for l in range(steps):
        sl = slice(l * nb, (l + 1) * nb)
        hr, hi = (a_re * hr - a_im * hi + bu_re[sl], a_re * hi + a_im * hr + bu_im[sl])
        hs_re.append(hr)
        hs_im.append(hi)
    h_re = jnp.concatenate(hs_re, axis=0).astype(BF16)
    h_im = jnp.concatenate(hs_im, axis=0).astype(BF16)
    ntt = (((1,), (1,)), ((), ()))
    y = (lax.dot_general(h_re, cr_ref[0].astype(BF16), ntt, preferred_element_type=F32)
         - lax.dot_general(h_im, ci_ref[0].astype(BF16), ntt, preferred_element_type=F32))
    y_ref[0] = y + d_ref[0] * u
    fr_ref[0] = hr
    fi_ref[0] = hi


def _ssm_sample(u_g, h0_re, h0_im, coefs, c_re, c_im, ssm_d, nb, steps):
    g, p, c = N_SSM_GROUPS, SSM_STATE, SSM_GROUP
    abar_re, abar_im = coefs[3].reshape(g, 1, p), coefs[4].reshape(g, 1, p)
    bb_re, bb_im = coefs[7].reshape(g, c, p), coefs[8].reshape(g, c, p)
    rows = steps * nb
    sds = jax.ShapeDtypeStruct
    spec = lambda a, b: pl.BlockSpec((1, a, b), lambda i: (i, 0, 0))
    return pl.pallas_call(
        functools.partial(_ssm_sample_kernel, nb=nb, steps=steps),
        grid=(g,),
        in_specs=[spec(rows, c), spec(nb, p), spec(nb, p), spec(1, p), spec(1, p),
                  spec(c, p), spec(c, p), spec(c, p), spec(c, p), spec(1, c)],
        out_specs=[spec(rows, c), spec(nb, p), spec(nb, p)],
        out_shape=[sds((g, rows, c), F32), sds((g, nb, p), F32), sds((g, nb, p), F32)],
        compiler_params=_cparams(("parallel",)),
        name="ssm_sample",
    )(u_g, h0_re, h0_im, abar_re, abar_im, bb_re, bb_im, c_re, c_im,
      ssm_d.reshape(g, 1, c))


def _layer_norm(x, g, b):
    mu = jnp.mean(x, axis=-1, keepdims=True)
    var = jnp.mean(jnp.square(x - mu), axis=-1, keepdims=True)
    return (x - mu) * lax.rsqrt(var + LN_EPS) * g + b


def _merge_kernel(o_ref, y_ref, x_ref, gw_ref, gb_ref, wo_ref, g_ref, b_ref,
                  x1_ref, x1t_ref):
    z = jax.nn.gelu(y_ref[...])
    gate = jnp.dot(z.astype(BF16), gw_ref[...], preferred_element_type=F32) + gb_ref[...]
    z = z * jax.nn.sigmoid(gate)
    mix = jnp.dot(o_ref[...].astype(BF16), wo_ref[0:ATTN_WIDTH, :], preferred_element_type=F32)
    mix = mix + jnp.dot(z.astype(BF16), wo_ref[ATTN_WIDTH:, :], preferred_element_type=F32)
    x1 = _layer_norm(DEEPNORM_ALPHA * x_ref[...] + mix, g_ref[...], b_ref[...])
    x1_ref[...] = x1
    x1t_ref[...] = x1.T.astype(BF16)


def _merge(attn_o, y_ssm, x, glu_w, glu_b, w_out, ln_g, ln_b, tm):
    t = x.shape[0]
    full = lambda r, c: pl.BlockSpec((r, c), lambda i: (0, 0))
    return pl.pallas_call(
        _merge_kernel,
        grid=(t // tm,),
        in_specs=[pl.BlockSpec((tm, ATTN_WIDTH), lambda i: (i, 0)),
                  pl.BlockSpec((tm, SSM_WIDTH), lambda i: (i, 0)),
                  pl.BlockSpec((tm, D_MODEL), lambda i: (i, 0)),
                  full(SSM_WIDTH, SSM_WIDTH), full(1, SSM_WIDTH),
                  full(D_MODEL, D_MODEL), full(1, D_MODEL), full(1, D_MODEL)],
        out_specs=[pl.BlockSpec((tm, D_MODEL), lambda i: (i, 0)),
                   pl.BlockSpec((D_MODEL, tm), lambda i: (0, i))],
        out_shape=[jax.ShapeDtypeStruct((t, D_MODEL), F32),
                   jax.ShapeDtypeStruct((D_MODEL, t), BF16)],
        compiler_params=_cparams(("parallel",)),
        name="merge_out_ln1",
    )(attn_o, y_ssm, x, glu_w, glu_b, w_out, ln_g, ln_b)


def _top_desc(vals, n):
    rows = []
    for r in range(n):
        mx = jnp.max(vals, axis=0, keepdims=True)
        rows.append(mx)
        if r + 1 < n:
            vals = jnp.where(vals == mx, -jnp.inf, vals)
    return rows


def _route_kernel(xt_ref, wq_ref, keys_ref, e1_ref, e2_ref, te_ref):
    k = PEER_TOPK
    qt = jnp.dot(wq_ref[...], xt_ref[...], preferred_element_type=F32).astype(BF16)
    for h in range(PEER_HEADS):
        tops = []
        for m in range(2):
            hm = 2 * h + m
            sc = jnp.dot(keys_ref[hm], qt[hm * PEER_NKEYS:(hm + 1) * PEER_NKEYS, :],
                         preferred_element_type=F32)
            tops.append((sc, _top_desc(sc, k + 1)))
        (sc1, a), (sc2, b) = tops
        ninf = jnp.full_like(a[0], -jnp.inf)
        pad = [ninf] * (-(k + 1) % 8)
        amat = jnp.concatenate(a + pad, axis=0)
        bmat = jnp.concatenate(b + pad, axis=0)
        row8 = lax.broadcasted_iota(jnp.int32, (8, amat.shape[1]), 0)
        groups = [a[0] + bmat]
        for r1 in range(2, 9):
            groups.append(jnp.where(row8 < (k + 1) // r1, a[r1 - 1] + bmat[0:8], -jnp.inf))
        groups.append(amat[8:] + b[0])
        cand = jnp.concatenate(groups, axis=0)
        best = _top_desc(cand, k + 1)
        zsum = jnp.ones_like(best[0])
        for r in range(1, k):
            zsum = zsum + jnp.exp(best[r] - best[0])
        theta = 0.5 * (best[k - 1] + best[k])
        e1_ref[h] = jnp.exp(sc1 - a[0]) / zsum
        e2_ref[h] = jnp.exp(sc2 - b[0])
        te_ref[h] = jnp.exp((theta - b[0]) - sc1)


def _route(x1t, wq_t, keys, tm):
    t = x1t.shape[1]
    big = pl.BlockSpec((PEER_HEADS, PEER_NKEYS, tm), lambda i: (0, 0, i))
    sds = jax.ShapeDtypeStruct((PEER_HEADS, PEER_NKEYS, t), F32)
    return pl.pallas_call(
        _route_kernel,
        grid=(t // tm,),
        in_specs=[pl.BlockSpec((D_MODEL, tm), lambda i: (0, i)),
                  pl.BlockSpec(wq_t.shape, lambda i: (0, 0)),
                  pl.BlockSpec(keys.shape, lambda i: (0, 0, 0))],
        out_specs=[big, big, big],
        out_shape=[sds, sds, sds],
        compiler_params=_cparams(("parallel",)),
        name="peer_route",
    )(x1t, wq_t, keys)


IPB = 4
UNIT_I = 2
ATTN_ROW_CHUNK = 512


def _peer_kernel(xt_ref, u_ref, vt_ref, e1_ref, e2_ref, te_ref, o_ref, *, tsub):
    eb = pl.program_id(1)

    @pl.when(eb == 0)
    def _zero():
        o_ref[...] = jnp.zeros(o_ref.shape, F32)

    n_sub = xt_ref.shape[1] // tsub
    upb = IPB // UNIT_I
    units = [(j, un) for j in range(n_sub) for un in range(upb)]
    urows = UNIT_I * PEER_NKEYS
    vrows = D_MODEL // upb

    def up(j, un):
        return jnp.dot(u_ref[un * urows:(un + 1) * urows, :],
                       xt_ref[:, j * tsub:(j + 1) * tsub], preferred_element_type=F32)

    def down_piece(j, a, piece):
        rows = slice(piece * vrows, (piece + 1) * vrows)
        o_ref[rows, j * tsub:(j + 1) * tsub] += jnp.dot(vt_ref[rows, :], a,
                                                        preferred_element_type=F32)

    def gate(j, ii, ht):
        i = eb * IPB + ii
        cols = slice(j * tsub, (j + 1) * tsub)
        e1rows = [e1_ref[h, pl.ds(i, 1), cols] for h in range(PEER_HEADS)]
        terows = [te_ref[h, pl.ds(i, 1), cols] for h in range(PEER_HEADS)]
        a_cols = []
        for c0 in range(0, tsub, LANES):
            hl = slice(c0, c0 + LANES)
            lc = slice(j * tsub + c0, j * tsub + c0 + LANES)
            w = None
            for h in range(PEER_HEADS):
                e2 = e2_ref[h, :, lc]
                wh = jnp.where(e2 >= terows[h][:, hl], e2, 0.0) * e1rows[h][:, hl]
                w = wh if w is None else w + wh
            a_cols.append((w * jax.nn.gelu(ht[:, hl])).astype(BF16))
        return a_cols[0] if len(a_cols) == 1 else jnp.concatenate(a_cols, axis=1)

    ht_next = up(*units[0])
    a_prev, a_rows = None, []
    for n, (j, un) in enumerate(units):
        ht = ht_next
        if n + 1 < len(units):
            ht_next = up(*units[n + 1])
        if a_prev is not None:
            down_piece(j - 1, a_prev, un)
        for k in range(UNIT_I):
            a_rows.append(gate(j, un * UNIT_I + k, ht[k * PEER_NKEYS:(k + 1) * PEER_NKEYS, :]))
        if un == upb - 1:
            a_prev, a_rows = jnp.concatenate(a_rows, axis=0), []
    for piece in range(upb):
        down_piece(n_sub - 1, a_prev, piece)


def _peer(x1t, u_tab, vt_tab, e1, e2, te_thr, tm, tsub):
    t = x1t.shape[1]
    te = IPB * PEER_NKEYS
    n_eb = u_tab.shape[0] // te
    once = pl.Buffered(1)
    big = pl.BlockSpec((PEER_HEADS, PEER_NKEYS, tm), lambda i, e: (0, 0, i), pipeline_mode=once)
    return pl.pallas_call(
        functools.partial(_peer_kernel, tsub=tsub),
        grid=(t // tm, n_eb),
        in_specs=[pl.BlockSpec((D_MODEL, tm), lambda i, e: (0, i), pipeline_mode=once),
                  pl.BlockSpec((te, D_MODEL), lambda i, e: (e, 0)),
                  pl.BlockSpec((D_MODEL, te), lambda i, e: (0, e)),
                  big, big, big],
        out_specs=pl.BlockSpec((D_MODEL, tm), lambda i, e: (0, i)),
        out_shape=jax.ShapeDtypeStruct((D_MODEL, t), F32),
        compiler_params=_cparams(("parallel", "arbitrary")),
        name="peer_experts",
    )(x1t, u_tab, vt_tab, e1, e2, te_thr)


def _ln2_kernel(x1_ref, ft_ref, g_ref, b_ref, o_ref):
    o_ref[...] = _layer_norm(DEEPNORM_ALPHA * x1_ref[...] + ft_ref[...].T,
                             g_ref[...], b_ref[...])


def _ln2(x1, ffn_t, ln_g, ln_b, tm):
    t = x1.shape[0]
    return pl.pallas_call(
        _ln2_kernel,
        grid=(t // tm,),
        in_specs=[pl.BlockSpec((tm, D_MODEL), lambda i: (i, 0)),
                  pl.BlockSpec((D_MODEL, tm), lambda i: (0, i)),
                  pl.BlockSpec((1, D_MODEL), lambda i: (0, 0)),
                  pl.BlockSpec((1, D_MODEL), lambda i: (0, 0))],
        out_specs=pl.BlockSpec((tm, D_MODEL), lambda i: (i, 0)),
        out_shape=jax.ShapeDtypeStruct((t, D_MODEL), F32),
        compiler_params=_cparams(("parallel",)),
        name="ln2",
    )(x1, ffn_t, ln_g, ln_b)


def _sample_buckets(past_len, dec_seq):
    qpos = past_len + np.tile(np.arange(dec_seq), 2)[:, None]
    w = PPS * PAGE_SIZE
    kpos = (past_len - w) + np.arange(w)[None, :]
    last = _bucket_np(qpos - kpos)
    jn = np.arange(NEWK)[None, :]
    knew = past_len + jn
    new = np.where((jn < dec_seq) & (knew <= qpos), _bucket_np(qpos - knew), -1)
    return last[None].astype(np.int32), new[None].astype(np.int32)


def _group_tail(x1, x1t, w_query_t, keys, u_tab, vt_tab, ln_g, ln_b, tm_route, tm_peer, tm_ln):
    e1, e2, te_thr = _route(x1t, w_query_t, keys, tm_route)
    ffn_t = _peer(x1t, u_tab, vt_tab, e1, e2, te_thr, tm_peer, min(tm_peer, 256))
    return _ln2(x1, ffn_t, ln_g, ln_b, tm_ln)


def kernel(x_prompt, x_sample, cache_k, cache_v, state_ssm_re, state_ssm_im, page_table, w_in, lambda_q1, lambda_k1, lambda_q2, lambda_k2, attn_subln_g, rel_bias, ssm_a_re, ssm_a_im, ssm_b_re, ssm_b_im, ssm_c_re, ssm_c_im, ssm_d, ssm_log_dt, ssm_glu_w, ssm_glu_b, w_out, ln1_g, ln1_b, peer_w_query, peer_sub_keys, peer_u, peer_v, ln2_g, ln2_b):
    l = 0
    seq = x_prompt.shape[1]
    dec_b, dec_s = x_sample.shape[0], x_sample.shape[1]
    n_pages = page_table.shape[1]
    past_len = n_pages * PAGE_SIZE
    g, p, c = N_SSM_GROUPS, SSM_STATE, SSM_GROUP
    lam_init = 0.8 - 0.6 * math.exp(-0.3 * l)
    lam = (jnp.exp(jnp.sum(lambda_q1[l] * lambda_k1[l]))
           - jnp.exp(jnp.sum(lambda_q2[l] * lambda_k2[l])) + lam_init).reshape(1).astype(F32)

    w_in_b = w_in[l].astype(BF16)
    glu_w_b = ssm_glu_w[l].astype(BF16)
    w_out_b = w_out[l].astype(BF16)
    wq_t = peer_w_query[l].T.astype(BF16)
    keys = peer_sub_keys[l].reshape(2 * PEER_HEADS, PEER_NKEYS, PEER_NKEYS).astype(BF16)
    u_tab = peer_u[l].astype(BF16)
    vt_tab = peer_v[l].T.astype(BF16)
    subln_g = attn_subln_g[l].reshape(1, HEAD_DIM)
    glu_b = ssm_glu_b[l].reshape(1, SSM_WIDTH)
    g1, b1 = ln1_g[l].reshape(1, D_MODEL), ln1_b[l].reshape(1, D_MODEL)
    g2, b2 = ln2_g[l].reshape(1, D_MODEL), ln2_b[l].reshape(1, D_MODEL)
    coefs = _ssm_coefs(ssm_log_dt[l], ssm_a_re[l], ssm_a_im[l],
                       ssm_b_re[l].transpose(0, 2, 1), ssm_b_im[l].transpose(0, 2, 1),
                       ssm_c_re[l], ssm_c_im[l])

    xp = x_prompt.reshape(seq, D_MODEL)
    tb = 1024
    q_p, k_p, v_p, u_p = _proj(xp, w_in_b, 512, ATTN_WIDTH)
    bias_p = _band_bias(rel_bias, tb, LOG2E)
    attn_p = _attn_prompt(q_p, k_p, v_p, bias_p, lam, subln_g, tb, lam_init)
    y_p, hr_p, hi_p = _ssm_prompt(u_p, coefs, ssm_d[l])
    x1_p, x1t_p = _merge(attn_p, y_p, xp, glu_w_b, glu_b, w_out_b, g1, b1, 256)
    out_p = _group_tail(x1_p, x1t_p, wq_t, keys, u_tab, vt_tab, g2, b2, 256, 1024, 256)

    ts = dec_b * dec_s
    xs = x_sample.reshape(ts, D_MODEL)
    qkvu_s = _proj(xs, w_in_b, ts, ATTN_WIDTH)
    q_s = qkvu_s[0].reshape(dec_b, dec_s, N_HEADS, HEAD_DIM) * (HALF_DIM ** -0.5)
    lane = np.arange(HEAD_DIM)
    halves = jnp.asarray(np.stack([lane < HALF_DIM, lane >= HALF_DIM]).astype(np.float32))
    q2 = (q_s[:, None] * halves[None, :, None, None, :])
    q2 = q2.transpose(0, 3, 1, 2, 4).reshape(dec_b, N_HEADS, QROWS, HEAD_DIM)
    pad_new = lambda a: jnp.pad(
        a.reshape(dec_b, dec_s, N_HEADS, HEAD_DIM).transpose(0, 2, 1, 3),
        ((0, 0), (0, 0), (0, NEWK - dec_s), (0, 0)))
    bl, bn = _sample_buckets(past_len, dec_s)
    bias_last = _bias_tiles(rel_bias, bl)[:, 0]
    bias_new = _bias_tiles(rel_bias, bn)[:, 0]
    attn_s = _attn_sample(q2, pad_new(qkvu_s[1]), pad_new(qkvu_s[2]), cache_k[l], cache_v[l],
                          page_table, bias_last, bias_new, lam, subln_g, lam_init)
    u_s = qkvu_s[3].reshape(dec_b, dec_s, g, c).transpose(2, 1, 0, 3).reshape(g, ts, c)
    y_g, hr_s, hi_s = _ssm_sample(u_s, state_ssm_re[l].transpose(1, 0, 2),
                                  state_ssm_im[l].transpose(1, 0, 2), coefs,
                                  ssm_c_re[l], ssm_c_im[l], ssm_d[l], dec_b, dec_s)
    y_s = y_g.reshape(g, dec_s, dec_b, c).transpose(2, 1, 0, 3).reshape(ts, SSM_WIDTH)
    x1_s, x1t_s = _merge(attn_s.reshape(ts, ATTN_WIDTH), y_s, xs, glu_w_b, glu_b, w_out_b,
                         g1, b1, ts)
    out_s = _group_tail(x1_s, x1t_s, wq_t, keys, u_tab, vt_tab, g2, b2, ts, ts, ts)

    kv = lambda a, b_, s_: a.reshape(1, b_, s_, N_HEADS, HEAD_DIM)
    return (out_p.reshape(1, seq, D_MODEL), out_s.reshape(dec_b, dec_s, D_MODEL),
            kv(k_p, 1, seq), kv(v_p, 1, seq),
            hr_p.reshape(1, 1, g, p), hi_p.reshape(1, 1, g, p),
            kv(qkvu_s[1], dec_b, dec_s), kv(qkvu_s[2], dec_b, dec_s),
            hr_s.transpose(1, 0, 2)[None], hi_s.transpose(1, 0, 2)[None])
```

```python
import functools
import math

import numpy as np
import jax
import jax.numpy as jnp
from jax import lax
from jax.experimental import pallas as pl
from jax.experimental.pallas import tpu as pltpu

F32 = jnp.float32
BF16 = jnp.bfloat16

D_MODEL = 2048
PAGE_SIZE = 128
ATTN_WIDTH = 1024
SSM_WIDTH = 1024
N_HEADS = 8
HEAD_DIM = 128
HALF_DIM = 64
N_BUCKETS = 32
MAX_DISTANCE = 128
SSM_GROUP = 16
N_SSM_GROUPS = 64
SSM_STATE = 64
PEER_HEADS = 8
PEER_NKEYS = 128
PEER_TOPK = 16
LN_EPS = 1e-5
DEPTH = 1
DEEPNORM_ALPHA = (2.0 * DEPTH) ** 0.25

LANES = 128
SCAN_CHUNK = 8
GBLK = LANES // SSM_GROUP
NEG = -1e30
LOG2E = 1.4426950408889634
VMEM_LIMIT = 56 * 1024 * 1024

_HI = lax.Precision.HIGHEST


def _cparams(sem):
    return pltpu.CompilerParams(dimension_semantics=sem, vmem_limit_bytes=VMEM_LIMIT)


def _rep(x, n):
    return x if n == 1 else jnp.concatenate([x] * n, axis=1)


def _proj_kernel(x_ref, w_ref, *o_refs):
    xb = x_ref[...].astype(BF16)
    tn = o_refs[0].shape[1]
    for j, o_ref in enumerate(o_refs):
        o_ref[...] = jnp.dot(xb, w_ref[:, j * tn:(j + 1) * tn], preferred_element_type=F32)


def _proj(x, w, tm, tn):
    t, k = x.shape
    n = w.shape[1]
    return pl.pallas_call(
        _proj_kernel,
        grid=(t // tm,),
        in_specs=[pl.BlockSpec((tm, k), lambda i: (i, 0)),
                  pl.BlockSpec((k, n), lambda i: (0, 0), pipeline_mode=pl.Buffered(1))],
        out_specs=[pl.BlockSpec((tm, tn), lambda i: (i, 0))] * (n // tn),
        out_shape=[jax.ShapeDtypeStruct((t, tn), F32)] * (n // tn),
        compiler_params=_cparams(("parallel",)),
        name="in_proj",
    )(x, w)


def _bucket_np(d):
    n = np.maximum(d, 0)
    max_exact = N_BUCKETS // 2
    nf = np.maximum(n, 1).astype(np.float64)
    large = max_exact + (np.log(nf / max_exact) / math.log(MAX_DISTANCE / max_exact)
                         * (N_BUCKETS - max_exact)).astype(np.int32)
    large = np.minimum(large, N_BUCKETS - 1)
    return np.where(n < max_exact, n, large).astype(np.int32)


def _bias_kernel(rb_ref, bkt_ref, o_ref, *, scale):
    h = pl.program_id(0)
    bkt = bkt_ref[0]
    far = rb_ref[N_BUCKETS - 1, h]
    acc = jnp.where(bkt < 0, NEG, 0.0).astype(F32)
    for b in range(N_BUCKETS - 1):
        acc = jnp.where(bkt == b, (rb_ref[b, h] - far) * scale, acc)
    o_ref[0, 0] = acc


def _bias_tiles(rel_bias, bkt, scale=1.0):
    n, r, c = bkt.shape
    return pl.pallas_call(
        functools.partial(_bias_kernel, scale=scale),
        grid=(N_HEADS, n),
        in_specs=[pl.BlockSpec(memory_space=pltpu.SMEM),
                  pl.BlockSpec((1, r, c), lambda h, i: (i, 0, 0))],
        out_specs=pl.BlockSpec((1, 1, r, c), lambda h, i: (h, i, 0, 0)),
        out_shape=jax.ShapeDtypeStruct((N_HEADS, n, r, c), F32),
        compiler_params=_cparams(("parallel", "arbitrary")),
        name="bias_tiles",
    )(rel_bias, jnp.asarray(bkt))


def _band_bias_kernel(rb_ref, bkt_ref, o_ref, *, tb, scale):
    h = pl.program_id(0)
    far = rb_ref[N_BUCKETS - 1, h]

    def block(bkt):
        acc = jnp.where(bkt < 0, NEG, 0.0).astype(F32)
        for b in range(N_BUCKETS - 1):
            acc = jnp.where(bkt == b, (rb_ref[b, h] - far) * scale, acc)
        return acc

    on_diag, below = block(bkt_ref[0]), block(bkt_ref[1])
    zero = jnp.zeros((LANES, LANES), F32)
    masked = jnp.full((LANES, LANES), NEG, F32)
    nblk = tb // LANES
    for a in range(nblk):
        row = [on_diag if b == a else below if b == a - 1 else zero if b < a else masked
               for b in range(nblk)]
        o_ref[0, 0, a * LANES:(a + 1) * LANES, :] = jnp.concatenate(row, axis=1)
    o_ref[0, 1] = jnp.zeros((tb, tb), F32)
    o_ref[0, 1, 0:LANES, tb - LANES:tb] = below


def _band_bias(rel_bias, tb, scale):
    assert MAX_DISTANCE <= LANES and tb % LANES == 0
    r = np.arange(LANES)[:, None]
    c = np.arange(LANES)[None, :]
    bkt = np.stack([np.where(c <= r, _bucket_np(r - c), -1), _bucket_np(r - c + LANES)])
    return pl.pallas_call(
        functools.partial(_band_bias_kernel, tb=tb, scale=scale),
        grid=(N_HEADS,),
        in_specs=[pl.BlockSpec(memory_space=pltpu.SMEM),
                  pl.BlockSpec((2, LANES, LANES), lambda h: (0, 0, 0))],
        out_specs=pl.BlockSpec((1, 2, tb, tb), lambda h: (h, 0, 0, 0)),
        out_shape=jax.ShapeDtypeStruct((N_HEADS, 2, tb, tb), F32),
        compiler_params=_cparams(("parallel",)),
        name="band_bias",
    )(rel_bias, jnp.asarray(bkt.astype(np.int32)))


def _subln(attn, g_row, lam_init):
    ms = jnp.mean(jnp.square(attn), axis=-1, keepdims=True)
    return attn * lax.rsqrt(ms + LN_EPS) * g_row * (1.0 - lam_init)


def _attn_kernel(qi_ref, ki_ref, lam_ref, q_ref, k_ref, v_ref, bias_ref, g_ref, o_ref,
                 q2_sc, m_sc, l_sc, acc_sc, *, tb, lam_init):
    s = pl.program_id(1)
    qi = qi_ref[s]
    ki = ki_ref[s]

    @pl.when(ki == 0)
    def _init():
        q = q_ref[...] * (HALF_DIM ** -0.5 * LOG2E)
        lane = lax.broadcasted_iota(jnp.int32, q.shape, 1)
        q2_sc[0:tb, :] = jnp.where(lane < HALF_DIM, q, 0.0).astype(BF16)
        q2_sc[tb:2 * tb, :] = jnp.where(lane >= HALF_DIM, q, 0.0).astype(BF16)
        m_sc[...] = jnp.full(m_sc.shape, NEG, F32)
        l_sc[...] = jnp.zeros(l_sc.shape, F32)
        acc_sc[...] = jnp.zeros(acc_sc.shape, F32)

    nt = (((1,), (1,)), ((), ()))
    rc = min(tb, ATTN_ROW_CHUNK)

    def update(bias_tile, causal):
        kb = k_ref[...].astype(BF16)
        vb = v_ref[...].astype(BF16)
        n_chunks = 2 * tb // rc

        def width(c):
            return (c * rc) % tb + rc if causal else tb

        def qk(c):
            w = width(c)
            sc = lax.dot_general(q2_sc[c * rc:(c + 1) * rc, :], kb[0:w], nt,
                                 preferred_element_type=F32)
            if bias_tile is not None:
                r0 = (c * rc) % tb
                sc = sc + bias_ref[0, bias_tile, r0:r0 + rc, 0:w]
            return sc

        def softmax(c, sc):
            rows = slice(c * rc, (c + 1) * rc)
            m_prev = m_sc[rows]
            m_new = jnp.maximum(m_prev, jnp.max(sc, axis=-1, keepdims=True))
            alpha = jnp.exp2(m_prev - m_new)
            p = jnp.exp2(sc - _rep(m_new, sc.shape[1] // LANES))
            l_sc[rows] = alpha * l_sc[rows] + jnp.sum(p, axis=-1, keepdims=True)
            m_sc[rows] = m_new
            return p.astype(BF16), alpha

        def pv(c, p, alpha):
            rows = slice(c * rc, (c + 1) * rc)
            acc_sc[rows] = alpha * acc_sc[rows] + jnp.dot(p, vb[0:width(c)],
                                                          preferred_element_type=F32)

        sc_next = qk(0)
        for c in range(n_chunks):
            sc = sc_next
            if c + 1 < n_chunks:
                sc_next = qk(c + 1)
            p, alpha = softmax(c, sc)
            pv(c, p, alpha)

    @pl.when(ki < qi - 1)
    def _far():
        update(None, False)

    @pl.when(ki == qi - 1)
    def _sub():
        update(1, False)

    @pl.when(ki == qi)
    def _diag():
        update(0, True)
        o = acc_sc[...] / l_sc[...]
        attn = o[0:tb] - lam_ref[0] * o[tb:2 * tb]
        o_ref[...] = _subln(attn, g_ref[...], lam_init)


def _attn_prompt(q, k, v, bias, lam, subln_g, tb, lam_init):
    t = q.shape[0]
    nb = t // tb
    qi_idx = np.concatenate([np.full(i + 1, i) for i in range(nb)]).astype(np.int32)
    ki_idx = np.concatenate([np.arange(i + 1) for i in range(nb)]).astype(np.int32)
    grid_spec = pltpu.PrefetchScalarGridSpec(
        num_scalar_prefetch=2,
        grid=(N_HEADS, len(qi_idx)),
        in_specs=[
            pl.BlockSpec(memory_space=pltpu.SMEM),
            pl.BlockSpec((tb, HEAD_DIM), lambda h, s, qi, ki: (qi[s], h)),
            pl.BlockSpec((tb, HEAD_DIM), lambda h, s, qi, ki: (ki[s], h)),
            pl.BlockSpec((tb, HEAD_DIM), lambda h, s, qi, ki: (ki[s], h)),
            pl.BlockSpec((1, 2, tb, tb), lambda h, s, qi, ki: (h, 0, 0, 0)),
            pl.BlockSpec((1, HEAD_DIM), lambda h, s, qi, ki: (0, 0)),
        ],
        out_specs=pl.BlockSpec((tb, HEAD_DIM), lambda h, s, qi, ki: (qi[s], h)),
        scratch_shapes=[
            pltpu.VMEM((2 * tb, HEAD_DIM), BF16),
            pltpu.VMEM((2 * tb, LANES), F32),
            pltpu.VMEM((2 * tb, LANES), F32),
            pltpu.VMEM((2 * tb, HEAD_DIM), F32),
        ],
    )
    return pl.pallas_call(
        functools.partial(_attn_kernel, tb=tb, lam_init=lam_init),
        grid_spec=grid_spec,
        out_shape=jax.ShapeDtypeStruct((t, ATTN_WIDTH), F32),
        compiler_params=_cparams(("parallel", "arbitrary")),
        name="attn_prompt",
    )(jnp.asarray(qi_idx), jnp.asarray(ki_idx), lam, q, k, v, bias, subln_g)


PPS = 16
QROWS = 8
NEWK = 16


def _decode_kernel(pt_ref, lam_ref, q2_ref, kn_ref, vn_ref, bl_ref, bn_ref, g_ref, *rest,
                   n_steps, lam_init):
    k_refs = rest[0:PPS]
    v_refs = rest[PPS:2 * PPS]
    o_ref = rest[2 * PPS]
    m_sc, l_sc, acc_sc = rest[2 * PPS + 1:]
    p_idx = pl.program_id(1)
    last = p_idx == n_steps - 1

    @pl.when(p_idx == 0)
    def _init():
        m_sc[...] = jnp.full(m_sc.shape, NEG, F32)
        l_sc[...] = jnp.zeros(l_sc.shape, F32)
        acc_sc[...] = jnp.zeros(acc_sc.shape, F32)

    last_f = jnp.where(last, 1.0, 0.0).astype(F32)

    def softmax_step(h, sc):
        m_prev = m_sc[h]
        m_new = jnp.maximum(m_prev, jnp.max(sc, axis=-1, keepdims=True))
        alpha = jnp.exp(m_prev - m_new)
        p = jnp.exp(sc - _rep(m_new, sc.shape[1] // LANES) if sc.shape[1] >= LANES
                    else sc - m_new[:, 0:sc.shape[1]])
        l_sc[h] = alpha * l_sc[h] + jnp.sum(p, axis=-1, keepdims=True)
        m_sc[h] = m_new
        return p.astype(BF16), alpha

    def accumulate(h, p, alpha, vmat):
        acc_sc[h] = alpha * acc_sc[h] + jnp.dot(p, vmat, preferred_element_type=F32)

    def update(h, sc, vmat):
        p, alpha = softmax_step(h, sc)
        accumulate(h, p, alpha, vmat)

    def head_rows(refs, h):
        rows_h = pl.ds(h, PAGE_SIZE, stride=N_HEADS)
        return jnp.concatenate([r[0, rows_h, :] for r in refs], axis=0).astype(BF16)

    scores = []
    for h in range(N_HEADS):
        q2 = q2_ref[0, h].astype(BF16)
        sc = lax.dot_general(q2, head_rows(k_refs, h), (((1,), (1,)), ((), ())),
                             preferred_element_type=F32)
        scores.append(sc + bl_ref[h] * last_f)
    probs = [softmax_step(h, scores[h]) for h in range(N_HEADS)]
    for h in range(N_HEADS):
        accumulate(h, probs[h][0], probs[h][1], head_rows(v_refs, h))

    @pl.when(last)
    def _finish():
        for h in range(N_HEADS):
            q2 = q2_ref[0, h].astype(BF16)
            kn = kn_ref[0, h].astype(BF16)
            vn = vn_ref[0, h].astype(BF16)
            sc = lax.dot_general(q2, kn, (((1,), (1,)), ((), ())), preferred_element_type=F32)
            update(h, sc + bn_ref[h], vn)
            o = acc_sc[h] / l_sc[h]
            half = QROWS // 2
            attn = o[0:half] - lam_ref[0] * o[half:QROWS]
            o_ref[0, :, h * HEAD_DIM:(h + 1) * HEAD_DIM] = _subln(attn, g_ref[...], lam_init)


def _attn_sample(q2, k_new, v_new, cache_k, cache_v, page_table, bias_last, bias_new,
                 lam, subln_g, lam_init):
    nb, n_pages = page_table.shape
    n_steps = n_pages // PPS
    dec_seq = QROWS // 2
    pages = lambda cache: cache.reshape(cache.shape[0], PAGE_SIZE * N_HEADS, HEAD_DIM)

    def page_spec(j):
        return pl.BlockSpec(
            (1, PAGE_SIZE * N_HEADS, HEAD_DIM),
            lambda b, p, pt, j=j: (pt[b * n_pages + p * PPS + j], 0, 0))

    grid_spec = pltpu.PrefetchScalarGridSpec(
        num_scalar_prefetch=1,
        grid=(nb, n_steps),
        in_specs=[
            pl.BlockSpec(memory_space=pltpu.SMEM),
            pl.BlockSpec((1, N_HEADS, QROWS, HEAD_DIM), lambda b, p, pt: (b, 0, 0, 0)),
            pl.BlockSpec((1, N_HEADS, NEWK, HEAD_DIM), lambda b, p, pt: (b, 0, 0, 0)),
            pl.BlockSpec((1, N_HEADS, NEWK, HEAD_DIM), lambda b, p, pt: (b, 0, 0, 0)),
            pl.BlockSpec((N_HEADS, QROWS, PPS * PAGE_SIZE), lambda b, p, pt: (0, 0, 0)),
            pl.BlockSpec((N_HEADS, QROWS, NEWK), lambda b, p, pt: (0, 0, 0)),
            pl.BlockSpec((1, HEAD_DIM), lambda b, p, pt: (0, 0)),
        ] + [page_spec(j) for j in range(PPS)] + [page_spec(j) for j in range(PPS)],
        out_specs=pl.BlockSpec((1, dec_seq, ATTN_WIDTH), lambda b, p, pt: (b, 0, 0)),
        scratch_shapes=[
            pltpu.VMEM((N_HEADS, QROWS, LANES), F32),
            pltpu.VMEM((N_HEADS, QROWS, LANES), F32),
            pltpu.VMEM((N_HEADS, QROWS, HEAD_DIM), F32),
        ],
    )
    return pl.pallas_call(
        functools.partial(_decode_kernel, n_steps=n_steps, lam_init=lam_init),
        grid_spec=grid_spec,
        out_shape=jax.ShapeDtypeStruct((nb, dec_seq, ATTN_WIDTH), F32),
        compiler_params=_cparams(("parallel", "arbitrary")),
        name="attn_sample",
    )(page_table.reshape(-1), lam, q2, k_new, v_new, bias_last, bias_new, subln_g,
      *([pages(cache_k)] * PPS), *([pages(cache_v)] * PPS))


def _ssm_coef_kernel(ldt_ref, ar_ref, ai_ref, bt_re_ref, bt_im_ref, c_re_ref, c_im_ref,
                     t_ref, s_ref, m_ref,
                     abar_re_ref, abar_im_ref, apow_re_ref, apow_im_ref,
                     bb_re_ref, bb_im_ref):
    lc, c, p = SCAN_CHUNK, SSM_GROUP, SSM_STATE
    hw = GBLK * p
    dt = jnp.exp(ldt_ref[0])
    ar, ai = ar_ref[0], ai_ref[0]
    mag = jnp.exp(ar * dt)
    abar_re = mag * jnp.cos(ai * dt)
    abar_im = mag * jnp.sin(ai * dt)
    nr, ni = abar_re - 1.0, abar_im
    den = ar * ar + ai * ai
    f_re = ((nr * ar + ni * ai) / den)[:, None, :]
    f_im = ((ni * ar - nr * ai) / den)[:, None, :]
    bt_re, bt_im = bt_re_ref[0], bt_im_ref[0]
    bb_re = f_re * bt_re - f_im * bt_im
    bb_im = f_re * bt_im + f_im * bt_re
    c_re, c_im = c_re_ref[0], c_im_ref[0]

    pw = [(jnp.ones_like(abar_re), jnp.zeros_like(abar_re))]
    for _ in range(lc):
        pr, pi = pw[-1]
        pw.append((pr * abar_re - pi * abar_im, pr * abar_im + pi * abar_re))

    rid = lax.broadcasted_iota(jnp.int32, (LANES, hw), 0) // c
    cid = lax.broadcasted_iota(jnp.int32, (LANES, hw), 1) // p
    same_group = rid == cid

    def bdiag(x):
        flat = x.reshape(LANES, p)
        return jnp.where(same_group, jnp.concatenate([flat] * GBLK, axis=1), 0.0)

    def cmul(xr, xi, w):
        wr, wi = w[0][:, None, :], w[1][:, None, :]
        return xr * wr - xi * wi, xr * wi + xi * wr

    l_re, l_im = bdiag(bb_re), bdiag(bb_im)
    nt = (((1,), (1,)), ((), ()))
    zero_tile = jnp.zeros((LANES, LANES), BF16)
    bd = []
    for tau in range(lc + 1):
        cr, ci = cmul(c_re, c_im, pw[tau])
        r_re, r_im = bdiag(cr), bdiag(ci)
        if tau < lc:
            bd.append((lax.dot_general(l_re, r_re, nt, precision=_HI, preferred_element_type=F32)
                       - lax.dot_general(l_im, r_im, nt, precision=_HI,
                                         preferred_element_type=F32)).astype(BF16))
        if tau > 0:
            cols = slice((tau - 1) * LANES, tau * LANES)
            m_ref[0, 0:hw, cols] = r_re.T.astype(BF16)
            m_ref[0, hw:2 * hw, cols] = (-r_im).T.astype(BF16)
    for s in range(lc):
        rows = slice(s * LANES, (s + 1) * LANES)
        for r in range(lc):
            t_ref[0, rows, r * LANES:(r + 1) * LANES] = bd[r - s] if r >= s else zero_tile
        wr, wi = cmul(bb_re, bb_im, pw[lc - 1 - s])
        s_ref[0, rows, 0:hw] = bdiag(wr).astype(BF16)
        s_ref[0, rows, hw:2 * hw] = bdiag(wi).astype(BF16)
    abar_re_ref[0] = abar_re
    abar_im_ref[0] = abar_im
    apow_re_ref[0] = pw[lc][0]
    apow_im_ref[0] = pw[lc][1]
    bb_re_ref[0] = bb_re
    bb_im_ref[0] = bb_im


def _ssm_coefs(log_dt, a_re, a_im, bt_re, bt_im, c_re, c_im):
    g, p, c, lc = N_SSM_GROUPS, SSM_STATE, SSM_GROUP, SCAN_CHUNK
    nb, hw, w = g // GBLK, GBLK * p, lc * LANES
    vec = pl.BlockSpec((1, GBLK, p), lambda i: (i, 0, 0))
    mat = pl.BlockSpec((1, GBLK, c, p), lambda i: (i, 0, 0, 0))
    sds = jax.ShapeDtypeStruct
    blk = lambda a: a.reshape((nb, GBLK) + a.shape[1:])
    return pl.pallas_call(
        _ssm_coef_kernel,
        grid=(nb,),
        in_specs=[pl.BlockSpec((1, GBLK, 1), lambda i: (i, 0, 0)), vec, vec, mat, mat, mat, mat],
        out_specs=[pl.BlockSpec((1, w, w), lambda i: (i, 0, 0)),
                   pl.BlockSpec((1, w, 2 * hw), lambda i: (i, 0, 0)),
                   pl.BlockSpec((1, 2 * hw, w), lambda i: (i, 0, 0)),
                   vec, vec, vec, vec, mat, mat],
        out_shape=[sds((nb, w, w), BF16), sds((nb, w, 2 * hw), BF16), sds((nb, 2 * hw, w), BF16)]
        + [sds((nb, GBLK, p), F32)] * 4 + [sds((nb, GBLK, c, p), F32)] * 2,
        compiler_params=_cparams(("parallel",)),
        name="ssm_coefs",
    )(log_dt.reshape(nb, GBLK, 1), blk(a_re), blk(a_im), blk(bt_re), blk(bt_im),
      blk(c_re), blk(c_im))


def _chunk_rows(u_ref, lc):
    nk = u_ref.shape[0] // lc
    return jnp.concatenate([u_ref[pl.ds(s, nk, stride=lc), :] for s in range(lc)], axis=1)


def _ssm_local_kernel(u_ref, s_ref, o_ref, *, lc):
    o_ref[0] = jnp.dot(_chunk_rows(u_ref, lc).astype(BF16), s_ref[0],
                       preferred_element_type=F32)


def _ssm_local(u, smat, lc):
    t = u.shape[0]
    nb, w, n = smat.shape
    return pl.pallas_call(
        functools.partial(_ssm_local_kernel, lc=lc),
        grid=(nb,),
        in_specs=[pl.BlockSpec((t, LANES), lambda i: (0, i)),
                  pl.BlockSpec((1, w, n), lambda i: (i, 0, 0))],
        out_specs=pl.BlockSpec((1, t // lc, n), lambda i: (i, 0, 0)),
        out_shape=jax.ShapeDtypeStruct((nb, t // lc, n), F32),
        compiler_params=_cparams(("parallel",)),
        name="ssm_local",
    )(u, smat)


def _ssm_rec_kernel(s_ref, are_ref, aim_ref, h_ref, fin_ref, st_sc):
    kc = s_ref.shape[0]
    w = are_ref.shape[1]
    a_re = are_ref[...]
    a_im = aim_ref[...]

    @pl.when(pl.program_id(0) == 0)
    def _init():
        st_sc[...] = jnp.zeros(st_sc.shape, F32)

    def body(k, carry):
        hr, hi = carry
        h_ref[k] = jnp.concatenate([hr, hi], axis=1)
        sk = s_ref[k]
        return (a_re * hr - a_im * hi + sk[:, 0:w], a_re * hi + a_im * hr + sk[:, w:2 * w])

    hr, hi = lax.fori_loop(0, kc, body, (st_sc[0], st_sc[1]))
    st_sc[0] = hr
    st_sc[1] = hi
    fin_ref[...] = jnp.concatenate([hr, hi], axis=1)


def _ssm_rec(s_kg, apow_re, apow_im):
    nk, gp, w2 = s_kg.shape
    kc = min(nk, 256)
    return pl.pallas_call(
        _ssm_rec_kernel,
        grid=(nk // kc,),
        in_specs=[pl.BlockSpec((kc, gp, w2), lambda i: (i, 0, 0)),
                  pl.BlockSpec((gp, w2 // 2), lambda i: (0, 0)),
                  pl.BlockSpec((gp, w2 // 2), lambda i: (0, 0))],
        out_specs=[pl.BlockSpec((kc, gp, w2), lambda i: (i, 0, 0)),
                   pl.BlockSpec((gp, w2), lambda i: (0, 0))],
        out_shape=[jax.ShapeDtypeStruct((nk, gp, w2), F32),
                   jax.ShapeDtypeStruct((gp, w2), F32)],
        scratch_shapes=[pltpu.VMEM((2, gp, w2 // 2), F32)],
        compiler_params=_cparams(("arbitrary",)),
        name="ssm_rec",
    )(s_kg, apow_re, apow_im)


def _ssm_out_kernel(u_ref, t_ref, h_ref, m_ref, d_ref, o_ref, *, lc):
    nk = u_ref.shape[0] // lc
    y = jnp.dot(_chunk_rows(u_ref, lc).astype(BF16), t_ref[0], preferred_element_type=F32)
    y = y + jnp.dot(h_ref[0].astype(BF16), m_ref[0], preferred_element_type=F32)
    for r in range(lc):
        rows = pl.ds(r, nk, stride=lc)
        o_ref[rows, :] = y[:, r * LANES:(r + 1) * LANES] + d_ref[...] * u_ref[rows, :]


def _ssm_out(u, tmat, hst, mmat, d_row, lc):
    t = u.shape[0]
    nb, w, _ = tmat.shape
    hw = hst.shape[2]
    return pl.pallas_call(
        functools.partial(_ssm_out_kernel, lc=lc),
        grid=(nb,),
        in_specs=[pl.BlockSpec((t, LANES), lambda i: (0, i)),
                  pl.BlockSpec((1, w, w), lambda i: (i, 0, 0)),
                  pl.BlockSpec((1, t // lc, hw), lambda i: (i, 0, 0)),
                  pl.BlockSpec((1, hw, w), lambda i: (i, 0, 0)),
                  pl.BlockSpec((1, LANES), lambda i: (0, i))],
        out_specs=pl.BlockSpec((t, LANES), lambda i: (0, i)),
        out_shape=jax.ShapeDtypeStruct(u.shape, F32),
        compiler_params=_cparams(("parallel",)),
        name="ssm_out",
    )(u, tmat, hst, mmat, d_row)


def _ssm_prompt(u, coefs, ssm_d):
    tmat, smat, mmat, _, _, apow_re, apow_im, _, _ = coefs
    g, p, lc = N_SSM_GROUPS, SSM_STATE, SCAN_CHUNK
    nb = g // GBLK
    hw = GBLK * p
    s_loc = _ssm_local(u, smat, lc)
    hst, fin = _ssm_rec(s_loc.transpose(1, 0, 2), apow_re.reshape(nb, hw),
                        apow_im.reshape(nb, hw))
    y = _ssm_out(u, tmat, hst.transpose(1, 0, 2), mmat, ssm_d.reshape(1, SSM_WIDTH), lc)
    return y, fin[:, :hw].reshape(g, p), fin[:, hw:].reshape(g, p)


def _ssm_sample_kernel(u_ref, hr_ref, hi_ref, are_ref, aim_ref, bbr_ref, bbi_ref,
                       cr_ref, ci_ref, d_ref, y_ref, fr_ref, fi_ref, *, nb, steps):
    u = u_ref[0]
    nt = (((1,), (0,)), ((), ()))
    bu_re = lax.dot_general(u, bbr_ref[0], nt, precision=_HI, preferred_element_type=F32)
    bu_im = lax.dot_general(u, bbi_ref[0], nt, precision=_HI, preferred_element_type=F32)
    a_re, a_im = are_ref[0], aim_ref[0]
    hr, hi = hr_ref[0], hi_ref[0]
    hs_re, hs_im = [], []
    for l in range(steps):
        sl = slice(l * nb, (l + 1) * nb)
        hr, hi = (a_re * hr - a_im * hi + bu_re[sl], a_re * hi + a_im * hr + bu_im[sl])
        hs_re.append(hr)
        hs_im.append(hi)
    h_re = jnp.concatenate(hs_re, axis=0).astype(BF16)
    h_im = jnp.concatenate(hs_im, axis=0).astype(BF16)
    ntt = (((1,), (1,)), ((), ()))
    y = (lax.dot_general(h_re, cr_ref[0].astype(BF16), ntt, preferred_element_type=F32)
         - lax.dot_general(h_im, ci_ref[0].astype(BF16), ntt, preferred_element_type=F32))
    y_ref[0] = y + d_ref[0] * u
    fr_ref[0] = hr
    fi_ref[0] = hi


def _ssm_sample(u_g, h0_re, h0_im, coefs, c_re, c_im, ssm_d, nb, steps):
    g, p, c = N_SSM_GROUPS, SSM_STATE, SSM_GROUP
    abar_re, abar_im = coefs[3].reshape(g, 1, p), coefs[4].reshape(g, 1, p)
    bb_re, bb_im = coefs[7].reshape(g, c, p), coefs[8].reshape(g, c, p)
    rows = steps * nb
    sds = jax.ShapeDtypeStruct
    spec = lambda a, b: pl.BlockSpec((1, a, b), lambda i: (i, 0, 0))
    return pl.pallas_call(
        functools.partial(_ssm_sample_kernel, nb=nb, steps=steps),
        grid=(g,),
        in_specs=[spec(rows, c), spec(nb, p), spec(nb, p), spec(1, p), spec(1, p),
                  spec(c, p), spec(c, p), spec(c, p), spec(c, p), spec(1, c)],
        out_specs=[spec(rows, c), spec(nb, p), spec(nb, p)],
        out_shape=[sds((g, rows, c), F32), sds((g, nb, p), F32), sds((g, nb, p), F32)],
        compiler_params=_cparams(("parallel",)),
        name="ssm_sample",
    )(u_g, h0_re, h0_im, abar_re, abar_im, bb_re, bb_im, c_re, c_im,
      ssm_d.reshape(g, 1, c))


def _layer_norm(x, g, b):
    mu = jnp.mean(x, axis=-1, keepdims=True)
    var = jnp.mean(jnp.square(x - mu), axis=-1, keepdims=True)
    return (x - mu) * lax.rsqrt(var + LN_EPS) * g + b


def _merge_kernel(o_ref, y_ref, x_ref, gw_ref, gb_ref, wo_ref, g_ref, b_ref,
                  x1_ref, x1t_ref):
    z = jax.nn.gelu(y_ref[...])
    gate = jnp.dot(z.astype(BF16), gw_ref[...], preferred_element_type=F32) + gb_ref[...]
    z = z * jax.nn.sigmoid(gate)
    mix = jnp.dot(o_ref[...].astype(BF16), wo_ref[0:ATTN_WIDTH, :], preferred_element_type=F32)
    mix = mix + jnp.dot(z.astype(BF16), wo_ref[ATTN_WIDTH:, :], preferred_element_type=F32)
    x1 = _layer_norm(DEEPNORM_ALPHA * x_ref[...] + mix, g_ref[...], b_ref[...])
    x1_ref[...] = x1
    x1t_ref[...] = x1.T.astype(BF16)


def _merge(attn_o, y_ssm, x, glu_w, glu_b, w_out, ln_g, ln_b, tm):
    t = x.shape[0]
    full = lambda r, c: pl.BlockSpec((r, c), lambda i: (0, 0))
    return pl.pallas_call(
        _merge_kernel,
        grid=(t // tm,),
        in_specs=[pl.BlockSpec((tm, ATTN_WIDTH), lambda i: (i, 0)),
                  pl.BlockSpec((tm, SSM_WIDTH), lambda i: (i, 0)),
                  pl.BlockSpec((tm, D_MODEL), lambda i: (i, 0)),
                  full(SSM_WIDTH, SSM_WIDTH), full(1, SSM_WIDTH),
                  full(D_MODEL, D_MODEL), full(1, D_MODEL), full(1, D_MODEL)],
        out_specs=[pl.BlockSpec((tm, D_MODEL), lambda i: (i, 0)),
                   pl.BlockSpec((D_MODEL, tm), lambda i: (0, i))],
        out_shape=[jax.ShapeDtypeStruct((t, D_MODEL), F32),
                   jax.ShapeDtypeStruct((D_MODEL, t), BF16)],
        compiler_params=_cparams(("parallel",)),
        name="merge_out_ln1",
    )(attn_o, y_ssm, x, glu_w, glu_b, w_out, ln_g, ln_b)


def _top_desc(vals, n):
    rows = []
    for r in range(n):
        mx = jnp.max(vals, axis=0, keepdims=True)
        rows.append(mx)
        if r + 1 < n:
            vals = jnp.where(vals == mx, -jnp.inf, vals)
    return rows


def _route_kernel(xt_ref, wq_ref, keys_ref, e1_ref, e2_ref, te_ref):
    k = PEER_TOPK
    qt = jnp.dot(wq_ref[...], xt_ref[...], preferred_element_type=F32).astype(BF16)
    for h in range(PEER_HEADS):
        tops = []
        for m in range(2):
            hm = 2 * h + m
            sc = jnp.dot(keys_ref[hm], qt[hm * PEER_NKEYS:(hm + 1) * PEER_NKEYS, :],
                         preferred_element_type=F32)
            tops.append((sc, _top_desc(sc, k + 1)))
        (sc1, a), (sc2, b) = tops
        ninf = jnp.full_like(a[0], -jnp.inf)
        pad = [ninf] * (-(k + 1) % 8)
        amat = jnp.concatenate(a + pad, axis=0)
        bmat = jnp.concatenate(b + pad, axis=0)
        row8 = lax.broadcasted_iota(jnp.int32, (8, amat.shape[1]), 0)
        groups = [a[0] + bmat]
        for r1 in range(2, 9):
            groups.append(jnp.where(row8 < (k + 1) // r1, a[r1 - 1] + bmat[0:8], -jnp.inf))
        groups.append(amat[8:] + b[0])
        cand = jnp.concatenate(groups, axis=0)
        best = _top_desc(cand, k + 1)
        zsum = jnp.ones_like(best[0])
        for r in range(1, k):
            zsum = zsum + jnp.exp(best[r] - best[0])
        theta = 0.5 * (best[k - 1] + best[k])
        e1_ref[h] = jnp.exp(sc1 - a[0]) / zsum
        e2_ref[h] = jnp.exp(sc2 - b[0])
        te_ref[h] = jnp.exp((theta - b[0]) - sc1)


def _route(x1t, wq_t, keys, tm):
    t = x1t.shape[1]
    big = pl.BlockSpec((PEER_HEADS, PEER_NKEYS, tm), lambda i: (0, 0, i))
    sds = jax.ShapeDtypeStruct((PEER_HEADS, PEER_NKEYS, t), F32)
    return pl.pallas_call(
        _route_kernel,
        grid=(t // tm,),
        in_specs=[pl.BlockSpec((D_MODEL, tm), lambda i: (0, i)),
                  pl.BlockSpec(wq_t.shape, lambda i: (0, 0)),
                  pl.BlockSpec(keys.shape, lambda i: (0, 0, 0))],
        out_specs=[big, big, big],
        out_shape=[sds, sds, sds],
        compiler_params=_cparams(("parallel",)),
        name="peer_route",
    )(x1t, wq_t, keys)


IPB = 4
UNIT_I = 2
ATTN_ROW_CHUNK = 256


def _peer_kernel(xt_ref, u_ref, vt_ref, e1_ref, e2_ref, te_ref, o_ref, *, tsub):
    eb = pl.program_id(1)

    @pl.when(eb == 0)
    def _zero():
        o_ref[...] = jnp.zeros(o_ref.shape, F32)

    n_sub = xt_ref.shape[1] // tsub
    upb = IPB // UNIT_I
    units = [(j, un) for j in range(n_sub) for un in range(upb)]
    urows = UNIT_I * PEER_NKEYS
    vrows = D_MODEL // upb

    def up(j, un):
        return jnp.dot(u_ref[un * urows:(un + 1) * urows, :],
                       xt_ref[:, j * tsub:(j + 1) * tsub], preferred_element_type=F32)

    def down_piece(j, a, piece):
        rows = slice(piece * vrows, (piece + 1) * vrows)
        o_ref[rows, j * tsub:(j + 1) * tsub] += jnp.dot(vt_ref[rows, :], a,
                                                        preferred_element_type=F32)

    def gate(j, ii, ht):
        i = eb * IPB + ii
        cols = slice(j * tsub, (j + 1) * tsub)
        e1rows = [e1_ref[h, pl.ds(i, 1), cols] for h in range(PEER_HEADS)]
        terows = [te_ref[h, pl.ds(i, 1), cols] for h in range(PEER_HEADS)]
        a_cols = []
        for c0 in range(0, tsub, LANES):
            hl = slice(c0, c0 + LANES)
            lc = slice(j * tsub + c0, j * tsub + c0 + LANES)
            w = None
            for h in range(PEER_HEADS):
                e2 = e2_ref[h, :, lc]
                wh = jnp.where(e2 >= terows[h][:, hl], e2, 0.0) * e1rows[h][:, hl]
                w = wh if w is None else w + wh
            a_cols.append((w * jax.nn.gelu(ht[:, hl])).astype(BF16))
        return a_cols[0] if len(a_cols) == 1 else jnp.concatenate(a_cols, axis=1)

    ht_next = up(*units[0])
    a_prev, a_rows = None, []
    for n, (j, un) in enumerate(units):
        ht = ht_next
        if n + 1 < len(units):
            ht_next = up(*units[n + 1])
        if a_prev is not None:
            down_piece(j - 1, a_prev, un)
        for k in range(UNIT_I):
            a_rows.append(gate(j, un * UNIT_I + k, ht[k * PEER_NKEYS:(k + 1) * PEER_NKEYS, :]))
        if un == upb - 1:
            a_prev, a_rows = jnp.concatenate(a_rows, axis=0), []
    for piece in range(upb):
        down_piece(n_sub - 1, a_prev, piece)


def _peer(x1t, u_tab, vt_tab, e1, e2, te_thr, tm, tsub):
    t = x1t.shape[1]
    te = IPB * PEER_NKEYS
    n_eb = u_tab.shape[0] // te
    once = pl.Buffered(1)
    big = pl.BlockSpec((PEER_HEADS, PEER_NKEYS, tm), lambda i, e: (0, 0, i), pipeline_mode=once)
    return pl.pallas_call(
        functools.partial(_peer_kernel, tsub=tsub),
        grid=(t // tm, n_eb),
        in_specs=[pl.BlockSpec((D_MODEL, tm), lambda i, e: (0, i), pipeline_mode=once),
                  pl.BlockSpec((te, D_MODEL), lambda i, e: (e, 0)),
                  pl.BlockSpec((D_MODEL, te), lambda i, e: (0, e)),
                  big, big, big],
        out_specs=pl.BlockSpec((D_MODEL, tm), lambda i, e: (0, i)),
        out_shape=jax.ShapeDtypeStruct((D_MODEL, t), F32),
        compiler_params=_cparams(("parallel", "arbitrary")),
        name="peer_experts",
    )(x1t, u_tab, vt_tab, e1, e2, te_thr)


def _ln2_kernel(x1_ref, ft_ref, g_ref, b_ref, o_ref):
    o_ref[...] = _layer_norm(DEEPNORM_ALPHA * x1_ref[...] + ft_ref[...].T,
                             g_ref[...], b_ref[...])


def _ln2(x1, ffn_t, ln_g, ln_b, tm):
    t = x1.shape[0]
    return pl.pallas_call(
        _ln2_kernel,
        grid=(t // tm,),
        in_specs=[pl.BlockSpec((tm, D_MODEL), lambda i: (i, 0)),
                  pl.BlockSpec((D_MODEL, tm), lambda i: (0, i)),
                  pl.BlockSpec((1, D_MODEL), lambda i: (0, 0)),
                  pl.BlockSpec((1, D_MODEL), lambda i: (0, 0))],
        out_specs=pl.BlockSpec((tm, D_MODEL), lambda i: (i, 0)),
        out_shape=jax.ShapeDtypeStruct((t, D_MODEL), F32),
        compiler_params=_cparams(("parallel",)),
        name="ln2",
    )(x1, ffn_t, ln_g, ln_b)


def _sample_buckets(past_len, dec_seq):
    qpos = past_len + np.tile(np.arange(dec_seq), 2)[:, None]
    w = PPS * PAGE_SIZE
    kpos = (past_len - w) + np.arange(w)[None, :]
    last = _bucket_np(qpos - kpos)
    jn = np.arange(NEWK)[None, :]
    knew = past_len + jn
    new = np.where((jn < dec_seq) & (knew <= qpos), _bucket_np(qpos - knew), -1)
    return last[None].astype(np.int32), new[None].astype(np.int32)


def _group_tail(x1, x1t, w_query_t, keys, u_tab, vt_tab, ln_g, ln_b, tm_route, tm_peer, tm_ln):
    e1, e2, te_thr = _route(x1t, w_query_t, keys, tm_route)
    ffn_t = _peer(x1t, u_tab, vt_tab, e1, e2, te_thr, tm_peer, min(tm_peer, 256))
    return _ln2(x1, ffn_t, ln_g, ln_b, tm_ln)


def kernel(x_prompt, x_sample, cache_k, cache_v, state_ssm_re, state_ssm_im, page_table, w_in, lambda_q1, lambda_k1, lambda_q2, lambda_k2, attn_subln_g, rel_bias, ssm_a_re, ssm_a_im, ssm_b_re, ssm_b_im, ssm_c_re, ssm_c_im, ssm_d, ssm_log_dt, ssm_glu_w, ssm_glu_b, w_out, ln1_g, ln1_b, peer_w_query, peer_sub_keys, peer_u, peer_v, ln2_g, ln2_b):
    l = 0
    seq = x_prompt.shape[1]
    dec_b, dec_s = x_sample.shape[0], x_sample.shape[1]
    n_pages = page_table.shape[1]
    past_len = n_pages * PAGE_SIZE
    g, p, c = N_SSM_GROUPS, SSM_STATE, SSM_GROUP
    lam_init = 0.8 - 0.6 * math.exp(-0.3 * l)
    lam = (jnp.exp(jnp.sum(lambda_q1[l] * lambda_k1[l]))
           - jnp.exp(jnp.sum(lambda_q2[l] * lambda_k2[l])) + lam_init).reshape(1).astype(F32)

    w_in_b = w_in[l].astype(BF16)
    glu_w_b = ssm_glu_w[l].astype(BF16)
    w_out_b = w_out[l].astype(BF16)
    wq_t = peer_w_query[l].T.astype(BF16)
    keys = peer_sub_keys[l].reshape(2 * PEER_HEADS, PEER_NKEYS, PEER_NKEYS).astype(BF16)
    u_tab = peer_u[l].astype(BF16)
    vt_tab = peer_v[l].T.astype(BF16)
    subln_g = attn_subln_g[l].reshape(1, HEAD_DIM)
    glu_b = ssm_glu_b[l].reshape(1, SSM_WIDTH)
    g1, b1 = ln1_g[l].reshape(1, D_MODEL), ln1_b[l].reshape(1, D_MODEL)
    g2, b2 = ln2_g[l].reshape(1, D_MODEL), ln2_b[l].reshape(1, D_MODEL)
    coefs = _ssm_coefs(ssm_log_dt[l], ssm_a_re[l], ssm_a_im[l],
                       ssm_b_re[l].transpose(0, 2, 1), ssm_b_im[l].transpose(0, 2, 1),
                       ssm_c_re[l], ssm_c_im[l])

    xp = x_prompt.reshape(seq, D_MODEL)
    tb = 1024
    q_p, k_p, v_p, u_p = _proj(xp, w_in_b, 512, ATTN_WIDTH)
    bias_p = _band_bias(rel_bias, tb, LOG2E)
    attn_p = _attn_prompt(q_p, k_p, v_p, bias_p, lam, subln_g, tb, lam_init)
    y_p, hr_p, hi_p = _ssm_prompt(u_p, coefs, ssm_d[l])
    x1_p, x1t_p = _merge(attn_p, y_p, xp, glu_w_b, glu_b, w_out_b, g1, b1, 256)
    out_p = _group_tail(x1_p, x1t_p, wq_t, keys, u_tab, vt_tab, g2, b2, 256, 1024, 256)

    ts = dec_b * dec_s
    xs = x_sample.reshape(ts, D_MODEL)
    qkvu_s = _proj(xs, w_in_b, ts, ATTN_WIDTH)
    q_s = qkvu_s[0].reshape(dec_b, dec_s, N_HEADS, HEAD_DIM) * (HALF_DIM ** -0.5)
    lane = np.arange(HEAD_DIM)
    halves = jnp.asarray(np.stack([lane < HALF_DIM, lane >= HALF_DIM]).astype(np.float32))
    q2 = (q_s[:, None] * halves[None, :, None, None, :])
    q2 = q2.transpose(0, 3, 1, 2, 4).reshape(dec_b, N_HEADS, QROWS, HEAD_DIM)
    pad_new = lambda a: jnp.pad(
        a.reshape(dec_b, dec_s, N_HEADS, HEAD_DIM).transpose(0, 2, 1, 3),
        ((0, 0), (0, 0), (0, NEWK - dec_s), (0, 0)))
    bl, bn = _sample_buckets(past_len, dec_s)
    bias_last = _bias_tiles(rel_bias, bl)[:, 0]
    bias_new = _bias_tiles(rel_bias, bn)[:, 0]
    attn_s = _attn_sample(q2, pad_new(qkvu_s[1]), pad_new(qkvu_s[2]), cache_k[l], cache_v[l],
                          page_table, bias_last, bias_new, lam, subln_g, lam_init)
    u_s = qkvu_s[3].reshape(dec_b, dec_s, g, c).transpose(2, 1, 0, 3).reshape(g, ts, c)
    y_g, hr_s, hi_s = _ssm_sample(u_s, state_ssm_re[l].transpose(1, 0, 2),
                                  state_ssm_im[l].transpose(1, 0, 2), coefs,
                                  ssm_c_re[l], ssm_c_im[l], ssm_d[l], dec_b, dec_s)
    y_s = y_g.reshape(g, dec_s, dec_b, c).transpose(2, 1, 0, 3).reshape(ts, SSM_WIDTH)
    x1_s, x1t_s = _merge(attn_s.reshape(ts, ATTN_WIDTH), y_s, xs, glu_w_b, glu_b, w_out_b,
                         g1, b1, ts)
    out_s = _group_tail(x1_s, x1t_s, wq_t, keys, u_tab, vt_tab, g2, b2, ts, ts, ts)

    kv = lambda a, b_, s_: a.reshape(1, b_, s_, N_HEADS, HEAD_DIM)
    return (out_p.reshape(1, seq, D_MODEL), out_s.reshape(dec_b, dec_s, D_MODEL),
            kv(k_p, 1, seq), kv(v_p, 1, seq),
            hr_p.reshape(1, 1, g, p), hi_p.reshape(1, 1, g, p),
            kv(qkvu_s[1], dec_b, dec_s), kv(qkvu_s[2], dec_b, dec_s),
            hr_s.transpose(1, 0, 2)[None], hi_s.transpose(1, 0, 2)[None])
```

```python
import functools
import math

import numpy as np
import jax
import jax.numpy as jnp
from jax import lax
from jax.experimental import pallas as pl
from jax.experimental.pallas import tpu as pltpu

F32 = jnp.float32
BF16 = jnp.bfloat16

D_MODEL = 2048
PAGE_SIZE = 128
ATTN_WIDTH = 1024
SSM_WIDTH = 1024
N_HEADS = 8
HEAD_DIM = 128
HALF_DIM = 64
N_BUCKETS = 32
MAX_DISTANCE = 128
SSM_GROUP = 16
N_SSM_GROUPS = 64
SSM_STATE = 64
PEER_HEADS = 8
PEER_NKEYS = 128
PEER_TOPK = 16
LN_EPS = 1e-5
DEPTH = 1
DEEPNORM_ALPHA = (2.0 * DEPTH) ** 0.25

LANES = 128
SCAN_CHUNK = 8
GBLK = LANES // SSM_GROUP
NEG = -1e30
LOG2E = 1.4426950408889634
VMEM_LIMIT = 56 * 1024 * 1024

_HI = lax.Precision.HIGHEST


def _cparams(sem):
    return pltpu.CompilerParams(dimension_semantics=sem, vmem_limit_bytes=VMEM_LIMIT)


def _rep(x, n):
    return x if n == 1 else jnp.concatenate([x] * n, axis=1)


def _proj_kernel(x_ref, w_ref, *o_refs):
    xb = x_ref[...].astype(BF16)
    tn = o_refs[0].shape[1]
    for j, o_ref in enumerate(o_refs):
        o_ref[...] = jnp.dot(xb, w_ref[:, j * tn:(j + 1) * tn], preferred_element_type=F32)


def _proj(x, w, tm, tn):
    t, k = x.shape
    n = w.shape[1]
    return pl.pallas_call(
        _proj_kernel,
        grid=(t // tm,),
        in_specs=[pl.BlockSpec((tm, k), lambda i: (i, 0)),
                  pl.BlockSpec((k, n), lambda i: (0, 0), pipeline_mode=pl.Buffered(1))],
        out_specs=[pl.BlockSpec((tm, tn), lambda i: (i, 0))] * (n // tn),
        out_shape=[jax.ShapeDtypeStruct((t, tn), F32)] * (n // tn),
        compiler_params=_cparams(("parallel",)),
        name="in_proj",
    )(x, w)


def _bucket_np(d):
    n = np.maximum(d, 0)
    max_exact = N_BUCKETS // 2
    nf = np.maximum(n, 1).astype(np.float64)
    large = max_exact + (np.log(nf / max_exact) / math.log(MAX_DISTANCE / max_exact)
                         * (N_BUCKETS - max_exact)).astype(np.int32)
    large = np.minimum(large, N_BUCKETS - 1)
    return np.where(n < max_exact, n, large).astype(np.int32)


def _bias_kernel(rb_ref, bkt_ref, o_ref, *, scale):
    h = pl.program_id(0)
    bkt = bkt_ref[0]
    far = rb_ref[N_BUCKETS - 1, h]
    acc = jnp.where(bkt < 0, NEG, 0.0).astype(F32)
    for b in range(N_BUCKETS - 1):
        acc = jnp.where(bkt == b, (rb_ref[b, h] - far) * scale, acc)
    o_ref[0, 0] = acc


def _bias_tiles(rel_bias, bkt, scale=1.0):
    n, r, c = bkt.shape
    return pl.pallas_call(
        functools.partial(_bias_kernel, scale=scale),
        grid=(N_HEADS, n),
        in_specs=[pl.BlockSpec(memory_space=pltpu.SMEM),
                  pl.BlockSpec((1, r, c), lambda h, i: (i, 0, 0))],
        out_specs=pl.BlockSpec((1, 1, r, c), lambda h, i: (h, i, 0, 0)),
        out_shape=jax.ShapeDtypeStruct((N_HEADS, n, r, c), F32),
        compiler_params=_cparams(("parallel", "arbitrary")),
        name="bias_tiles",
    )(rel_bias, jnp.asarray(bkt))


def _band_bias_kernel(rb_ref, bkt_ref, o_ref, *, tb, scale):
    h = pl.program_id(0)
    far = rb_ref[N_BUCKETS - 1, h]

    def block(bkt):
        acc = jnp.where(bkt < 0, NEG, 0.0).astype(F32)
        for b in range(N_BUCKETS - 1):
            acc = jnp.where(bkt == b, (rb_ref[b, h] - far) * scale, acc)
        return acc

    on_diag, below = block(bkt_ref[0]), block(bkt_ref[1])
    zero = jnp.zeros((LANES, LANES), F32)
    masked = jnp.full((LANES, LANES), NEG, F32)
    nblk = tb // LANES
    for a in range(nblk):
        row = [on_diag if b == a else below if b == a - 1 else zero if b < a else masked
               for b in range(nblk)]
        o_ref[0, 0, a * LANES:(a + 1) * LANES, :] = jnp.concatenate(row, axis=1)
    o_ref[0, 1] = jnp.zeros((tb, tb), F32)
    o_ref[0, 1, 0:LANES, tb - LANES:tb] = below


def _band_bias(rel_bias, tb, scale):
    assert MAX_DISTANCE <= LANES and tb % LANES == 0
    r = np.arange(LANES)[:, None]
    c = np.arange(LANES)[None, :]
    bkt = np.stack([np.where(c <= r, _bucket_np(r - c), -1), _bucket_np(r - c + LANES)])
    return pl.pallas_call(
        functools.partial(_band_bias_kernel, tb=tb, scale=scale),
        grid=(N_HEADS,),
        in_specs=[pl.BlockSpec(memory_space=pltpu.SMEM),
                  pl.BlockSpec((2, LANES, LANES), lambda h: (0, 0, 0))],
        out_specs=pl.BlockSpec((1, 2, tb, tb), lambda h: (h, 0, 0, 0)),
        out_shape=jax.ShapeDtypeStruct((N_HEADS, 2, tb, tb), F32),
        compiler_params=_cparams(("parallel",)),
        name="band_bias",
    )(rel_bias, jnp.asarray(bkt.astype(np.int32)))


def _subln(attn, g_row, lam_init):
    ms = jnp.mean(jnp.square(attn), axis=-1, keepdims=True)
    return attn * lax.rsqrt(ms + LN_EPS) * g_row * (1.0 - lam_init)


def _attn_kernel(qi_ref, ki_ref, lam_ref, q_ref, k_ref, v_ref, bias_ref, g_ref, o_ref,
                 q2_sc, m_sc, l_sc, acc_sc, *, tb, lam_init):
    s = pl.program_id(1)
    qi = qi_ref[s]
    ki = ki_ref[s]

    @pl.when(ki == 0)
    def _init():
        q = q_ref[...] * (HALF_DIM ** -0.5 * LOG2E)
        lane = lax.broadcasted_iota(jnp.int32, q.shape, 1)
        q2_sc[0:tb, :] = jnp.where(lane < HALF_DIM, q, 0.0).astype(BF16)
        q2_sc[tb:2 * tb, :] = jnp.where(lane >= HALF_DIM, q, 0.0).astype(BF16)
        m_sc[...] = jnp.full(m_sc.shape, NEG, F32)
        l_sc[...] = jnp.zeros(l_sc.shape, F32)
        acc_sc[...] = jnp.zeros(acc_sc.shape, F32)

    nt = (((1,), (1,)), ((), ()))
    rc = min(tb, ATTN_ROW_CHUNK)

    def update(bias_tile, causal):
        kb = k_ref[...].astype(BF16)
        vb = v_ref[...].astype(BF16)
        n_chunks = 2 * tb // rc

        def width(c):
            return (c * rc) % tb + rc if causal else tb

        def qk(c):
            w = width(c)
            sc = lax.dot_general(q2_sc[c * rc:(c + 1) * rc, :], kb[0:w], nt,
                                 preferred_element_type=F32)
            if bias_tile is not None:
                r0 = (c * rc) % tb
                sc = sc + bias_ref[0, bias_tile, r0:r0 + rc, 0:w]
            return sc

        def softmax(c, sc):
            rows = slice(c * rc, (c + 1) * rc)
            m_prev = m_sc[rows]
            m_new = jnp.maximum(m_prev, jnp.max(sc, axis=-1, keepdims=True))
            alpha = jnp.exp2(m_prev - m_new)
            p = jnp.exp2(sc - _rep(m_new, sc.shape[1] // LANES))
            l_sc[rows] = alpha * l_sc[rows] + jnp.sum(p, axis=-1, keepdims=True)
            m_sc[rows] = m_new
            return p.astype(BF16), alpha

        def pv(c, p, alpha):
            rows = slice(c * rc, (c + 1) * rc)
            acc_sc[rows] = alpha * acc_sc[rows] + jnp.dot(p, vb[0:width(c)],
                                                          preferred_element_type=F32)

        sc_next = qk(0)
        for c in range(n_chunks):
            sc = sc_next
            if c + 1 < n_chunks:
                sc_next = qk(c + 1)
            p, alpha = softmax(c, sc)
            pv(c, p, alpha)

    @pl.when(ki < qi - 1)
    def _far():
        update(None, False)

    @pl.when(ki == qi - 1)
    def _sub():
        update(1, False)

    @pl.when(ki == qi)
    def _diag():
        update(0, True)
        o = acc_sc[...] / l_sc[...]
        attn = o[0:tb] - lam_ref[0] * o[tb:2 * tb]
        o_ref[...] = _subln(attn, g_ref[...], lam_init)


def _attn_prompt(q, k, v, bias, lam, subln_g, tb, lam_init):
    t = q.shape[0]
    nb = t // tb
    qi_idx = np.concatenate([np.full(i + 1, i) for i in range(nb)]).astype(np.int32)
    ki_idx = np.concatenate([np.arange(i + 1) for i in range(nb)]).astype(np.int32)
    grid_spec = pltpu.PrefetchScalarGridSpec(
        num_scalar_prefetch=2,
        grid=(N_HEADS, len(qi_idx)),
        in_specs=[
            pl.BlockSpec(memory_space=pltpu.SMEM),
            pl.BlockSpec((tb, HEAD_DIM), lambda h, s, qi, ki: (qi[s], h)),
            pl.BlockSpec((tb, HEAD_DIM), lambda h, s, qi, ki: (ki[s], h)),
            pl.BlockSpec((tb, HEAD_DIM), lambda h, s, qi, ki: (ki[s], h)),
            pl.BlockSpec((1, 2, tb, tb), lambda h, s, qi, ki: (h, 0, 0, 0)),
            pl.BlockSpec((1, HEAD_DIM), lambda h, s, qi, ki: (0, 0)),
        ],
        out_specs=pl.BlockSpec((tb, HEAD_DIM), lambda h, s, qi, ki: (qi[s], h)),
        scratch_shapes=[
            pltpu.VMEM((2 * tb, HEAD_DIM), BF16),
            pltpu.VMEM((2 * tb, LANES), F32),
            pltpu.VMEM((2 * tb, LANES), F32),
            pltpu.VMEM((2 * tb, HEAD_DIM), F32),
        ],
    )
    return pl.pallas_call(
        functools.partial(_attn_kernel, tb=tb, lam_init=lam_init),
        grid_spec=grid_spec,
        out_shape=jax.ShapeDtypeStruct((t, ATTN_WIDTH), F32),
        compiler_params=_cparams(("parallel", "arbitrary")),
        name="attn_prompt",
    )(jnp.asarray(qi_idx), jnp.asarray(ki_idx), lam, q, k, v, bias, subln_g)


PPS = 16
QROWS = 8
NEWK = 16


def _decode_kernel(pt_ref, lam_ref, q2_ref, kn_ref, vn_ref, bl_ref, bn_ref, g_ref, *rest,
                   n_steps, lam_init):
    k_refs = rest[0:PPS]
    v_refs = rest[PPS:2 * PPS]
    o_ref = rest[2 * PPS]
    m_sc, l_sc, acc_sc = rest[2 * PPS + 1:]
    p_idx = pl.program_id(1)
    last = p_idx == n_steps - 1

    @pl.when(p_idx == 0)
    def _init():
        m_sc[...] = jnp.full(m_sc.shape, NEG, F32)
        l_sc[...] = jnp.zeros(l_sc.shape, F32)
        acc_sc[...] = jnp.zeros(acc_sc.shape, F32)

    last_f = jnp.where(last, 1.0, 0.0).astype(F32)

    def softmax_step(h, sc):
        m_prev = m_sc[h]
        m_new = jnp.maximum(m_prev, jnp.max(sc, axis=-1, keepdims=True))
        alpha = jnp.exp(m_prev - m_new)
        p = jnp.exp(sc - _rep(m_new, sc.shape[1] // LANES) if sc.shape[1] >= LANES
                    else sc - m_new[:, 0:sc.shape[1]])
        l_sc[h] = alpha * l_sc[h] + jnp.sum(p, axis=-1, keepdims=True)
        m_sc[h] = m_new
        return p.astype(BF16), alpha

    def accumulate(h, p, alpha, vmat):
        acc_sc[h] = alpha * acc_sc[h] + jnp.dot(p, vmat, preferred_element_type=F32)

    def update(h, sc, vmat):
        p, alpha = softmax_step(h, sc)
        accumulate(h, p, alpha, vmat)

    def head_rows(refs, h):
        rows_h = pl.ds(h, PAGE_SIZE, stride=N_HEADS)
        return jnp.concatenate([r[0, rows_h, :] for r in refs], axis=0).astype(BF16)

    scores = []
    for h in range(N_HEADS):
        q2 = q2_ref[0, h].astype(BF16)
        sc = lax.dot_general(q2, head_rows(k_refs, h), (((1,), (1,)), ((), ())),
                             preferred_element_type=F32)
        scores.append(sc + bl_ref[h] * last_f)
    probs = [softmax_step(h, scores[h]) for h in range(N_HEADS)]
    for h in range(N_HEADS):
        accumulate(h, probs[h][0], probs[h][1], head_rows(v_refs, h))

    @pl.when(last)
    def _finish():
        for h in range(N_HEADS):
            q2 = q2_ref[0, h].astype(BF16)
            kn = kn_ref[0, h].astype(BF16)
            vn = vn_ref[0, h].astype(BF16)
            sc = lax.dot_general(q2, kn, (((1,), (1,)), ((), ())), preferred_element_type=F32)
            update(h, sc + bn_ref[h], vn)
            o = acc_sc[h] / l_sc[h]
            half = QROWS // 2
            attn = o[0:half] - lam_ref[0] * o[half:QROWS]
            o_ref[0, :, h * HEAD_DIM:(h + 1) * HEAD_DIM] = _subln(attn, g_ref[...], lam_init)


def _attn_sample(q2, k_new, v_new, cache_k, cache_v, page_table, bias_last, bias_new,
                 lam, subln_g, lam_init):
    nb, n_pages = page_table.shape
    n_steps = n_pages // PPS
    dec_seq = QROWS // 2
    pages = lambda cache: cache.reshape(cache.shape[0], PAGE_SIZE * N_HEADS, HEAD_DIM)

    def page_spec(j):
        return pl.BlockSpec(
            (1, PAGE_SIZE * N_HEADS, HEAD_DIM),
            lambda b, p, pt, j=j: (pt[b * n_pages + p * PPS + j], 0, 0))

    grid_spec = pltpu.PrefetchScalarGridSpec(
        num_scalar_prefetch=1,
        grid=(nb, n_steps),
        in_specs=[
            pl.BlockSpec(memory_space=pltpu.SMEM),
            pl.BlockSpec((1, N_HEADS, QROWS, HEAD_DIM), lambda b, p, pt: (b, 0, 0, 0)),
            pl.BlockSpec((1, N_HEADS, NEWK, HEAD_DIM), lambda b, p, pt: (b, 0, 0, 0)),
            pl.BlockSpec((1, N_HEADS, NEWK, HEAD_DIM), lambda b, p, pt: (b, 0, 0, 0)),
            pl.BlockSpec((N_HEADS, QROWS, PPS * PAGE_SIZE), lambda b, p, pt: (0, 0, 0)),
            pl.BlockSpec((N_HEADS, QROWS, NEWK), lambda b, p, pt: (0, 0, 0)),
            pl.BlockSpec((1, HEAD_DIM), lambda b, p, pt: (0, 0)),
        ] + [page_spec(j) for j in range(PPS)] + [page_spec(j) for j in range(PPS)],
        out_specs=pl.BlockSpec((1, dec_seq, ATTN_WIDTH), lambda b, p, pt: (b, 0, 0)),
        scratch_shapes=[
            pltpu.VMEM((N_HEADS, QROWS, LANES), F32),
            pltpu.VMEM((N_HEADS, QROWS, LANES), F32),
            pltpu.VMEM((N_HEADS, QROWS, HEAD_DIM), F32),
        ],
    )
    return pl.pallas_call(
        functools.partial(_decode_kernel, n_steps=n_steps, lam_init=lam_init),
        grid_spec=grid_spec,
        out_shape=jax.ShapeDtypeStruct((nb, dec_seq, ATTN_WIDTH), F32),
        compiler_params=_cparams(("parallel", "arbitrary")),
        name="attn_sample",
    )(page_table.reshape(-1), lam, q2, k_new, v_new, bias_last, bias_new, subln_g,
      *([pages(cache_k)] * PPS), *([pages(cache_v)] * PPS))


def _ssm_coef_kernel(ldt_ref, ar_ref, ai_ref, bt_re_ref, bt_im_ref, c_re_ref, c_im_ref,
                     t_ref, s_ref, m_ref,
                     abar_re_ref, abar_im_ref, apow_re_ref, apow_im_ref,
                     bb_re_ref, bb_im_ref):
    lc, c, p = SCAN_CHUNK, SSM_GROUP, SSM_STATE
    hw = GBLK * p
    dt = jnp.exp(ldt_ref[0])
    ar, ai = ar_ref[0], ai_ref[0]
    mag = jnp.exp(ar * dt)
    abar_re = mag * jnp.cos(ai * dt)
    abar_im = mag * jnp.sin(ai * dt)
    nr, ni = abar_re - 1.0, abar_im
    den = ar * ar + ai * ai
    f_re = ((nr * ar + ni * ai) / den)[:, None, :]
    f_im = ((ni * ar - nr * ai) / den)[:, None, :]
    bt_re, bt_im = bt_re_ref[0], bt_im_ref[0]
    bb_re = f_re * bt_re - f_im * bt_im
    bb_im = f_re * bt_im + f_im * bt_re
    c_re, c_im = c_re_ref[0], c_im_ref[0]

    pw = [(jnp.ones_like(abar_re), jnp.zeros_like(abar_re))]
    for _ in range(lc):
        pr, pi = pw[-1]
        pw.append((pr * abar_re - pi * abar_im, pr * abar_im + pi * abar_re))

    rid = lax.broadcasted_iota(jnp.int32, (LANES, hw), 0) // c
    cid = lax.broadcasted_iota(jnp.int32, (LANES, hw), 1) // p
    same_group = rid == cid

    def bdiag(x):
        flat = x.reshape(LANES, p)
        return jnp.where(same_group, jnp.concatenate([flat] * GBLK, axis=1), 0.0)

    def cmul(xr, xi, w):
        wr, wi = w[0][:, None, :], w[1][:, None, :]
        return xr * wr - xi * wi, xr * wi + xi * wr

    l_re, l_im = bdiag(bb_re), bdiag(bb_im)
    nt = (((1,), (1,)), ((), ()))
    zero_tile = jnp.zeros((LANES, LANES), BF16)
    bd = []
    for tau in range(lc + 1):
        cr, ci = cmul(c_re, c_im, pw[tau])
        r_re, r_im = bdiag(cr), bdiag(ci)
        if tau < lc:
            bd.append((lax.dot_general(l_re, r_re, nt, precision=_HI, preferred_element_type=F32)
                       - lax.dot_general(l_im, r_im, nt, precision=_HI,
                                         preferred_element_type=F32)).astype(BF16))
        if tau > 0:
            cols = slice((tau - 1) * LANES, tau * LANES)
            m_ref[0, 0:hw, cols] = r_re.T.astype(BF16)
            m_ref[0, hw:2 * hw, cols] = (-r_im).T.astype(BF16)
    for s in range(lc):
        rows = slice(s * LANES, (s + 1) * LANES)
        for r in range(lc):
            t_ref[0, rows, r * LANES:(r + 1) * LANES] = bd[r - s] if r >= s else zero_tile
        wr, wi = cmul(bb_re, bb_im, pw[lc - 1 - s])
        s_ref[0, rows, 0:hw] = bdiag(wr).astype(BF16)
        s_ref[0, rows, hw:2 * hw] = bdiag(wi).astype(BF16)
    abar_re_ref[0] = abar_re
    abar_im_ref[0] = abar_im
    apow_re_ref[0] = pw[lc][0]
    apow_im_ref[0] = pw[lc][1]
    bb_re_ref[0] = bb_re
    bb_im_ref[0] = bb_im


def _ssm_coefs(log_dt, a_re, a_im, bt_re, bt_im, c_re, c_im):
    g, p, c, lc = N_SSM_GROUPS, SSM_STATE, SSM_GROUP, SCAN_CHUNK
    nb, hw, w = g // GBLK, GBLK * p, lc * LANES
    vec = pl.BlockSpec((1, GBLK, p), lambda i: (i, 0, 0))
    mat = pl.BlockSpec((1, GBLK, c, p), lambda i: (i, 0, 0, 0))
    sds = jax.ShapeDtypeStruct
    blk = lambda a: a.reshape((nb, GBLK) + a.shape[1:])
    return pl.pallas_call(
        _ssm_coef_kernel,
        grid=(nb,),
        in_specs=[pl.BlockSpec((1, GBLK, 1), lambda i: (i, 0, 0)), vec, vec, mat, mat, mat, mat],
        out_specs=[pl.BlockSpec((1, w, w), lambda i: (i, 0, 0)),
                   pl.BlockSpec((1, w, 2 * hw), lambda i: (i, 0, 0)),
                   pl.BlockSpec((1, 2 * hw, w), lambda i: (i, 0, 0)),
                   vec, vec, vec, vec, mat, mat],
        out_shape=[sds((nb, w, w), BF16), sds((nb, w, 2 * hw), BF16), sds((nb, 2 * hw, w), BF16)]
        + [sds((nb, GBLK, p), F32)] * 4 + [sds((nb, GBLK, c, p), F32)] * 2,
        compiler_params=_cparams(("parallel",)),
        name="ssm_coefs",
    )(log_dt.reshape(nb, GBLK, 1), blk(a_re), blk(a_im), blk(bt_re), blk(bt_im),
      blk(c_re), blk(c_im))


def _chunk_rows(u_ref, lc):
    nk = u_ref.shape[0] // lc
    return jnp.concatenate([u_ref[pl.ds(s, nk, stride=lc), :] for s in range(lc)], axis=1)


def _ssm_local_kernel(u_ref, s_ref, o_ref, *, lc):
    o_ref[0] = jnp.dot(_chunk_rows(u_ref, lc).astype(BF16), s_ref[0],
                       preferred_element_type=F32)


def _ssm_local(u, smat, lc):
    t = u.shape[0]
    nb, w, n = smat.shape
    return pl.pallas_call(
        functools.partial(_ssm_local_kernel, lc=lc),
        grid=(nb,),
        in_specs=[pl.BlockSpec((t, LANES), lambda i: (0, i)),
                  pl.BlockSpec((1, w, n), lambda i: (i, 0, 0))],
        out_specs=pl.BlockSpec((1, t // lc, n), lambda i: (i, 0, 0)),
        out_shape=jax.ShapeDtypeStruct((nb, t // lc, n), F32),
        compiler_params=_cparams(("parallel",)),
        name="ssm_local",
    )(u, smat)


def _ssm_rec_kernel(s_ref, are_ref, aim_ref, h_ref, fin_ref, st_sc):
    kc = s_ref.shape[0]
    w = are_ref.shape[1]
    a_re = are_ref[...]
    a_im = aim_ref[...]

    @pl.when(pl.program_id(0) == 0)
    def _init():
        st_sc[...] = jnp.zeros(st_sc.shape, F32)

    def body(k, carry):
        hr, hi = carry
        h_ref[k] = jnp.concatenate([hr, hi], axis=1)
        sk = s_ref[k]
        return (a_re * hr - a_im * hi + sk[:, 0:w], a_re * hi + a_im * hr + sk[:, w:2 * w])

    hr, hi = lax.fori_loop(0, kc, body, (st_sc[0], st_sc[1]))
    st_sc[0] = hr
    st_sc[1] = hi
    fin_ref[...] = jnp.concatenate([hr, hi], axis=1)


def _ssm_rec(s_kg, apow_re, apow_im):
    nk, gp, w2 = s_kg.shape
    kc = min(nk, 256)
    return pl.pallas_call(
        _ssm_rec_kernel,
        grid=(nk // kc,),
        in_specs=[pl.BlockSpec((kc, gp, w2), lambda i: (i, 0, 0)),
                  pl.BlockSpec((gp, w2 // 2), lambda i: (0, 0)),
                  pl.BlockSpec((gp, w2 // 2), lambda i: (0, 0))],
        out_specs=[pl.BlockSpec((kc, gp, w2), lambda i: (i, 0, 0)),
                   pl.BlockSpec((gp, w2), lambda i: (0, 0))],
        out_shape=[jax.ShapeDtypeStruct((nk, gp, w2), F32),
                   jax.ShapeDtypeStruct((gp, w2), F32)],
        scratch_shapes=[pltpu.VMEM((2, gp, w2 // 2), F32)],
        compiler_params=_cparams(("arbitrary",)),
        name="ssm_rec",
    )(s_kg, apow_re, apow_im)


def _ssm_out_kernel(u_ref, t_ref, h_ref, m_ref, d_ref, o_ref, *, lc):
    nk = u_ref.shape[0] // lc
    y = jnp.dot(_chunk_rows(u_ref, lc).astype(BF16), t_ref[0], preferred_element_type=F32)
    y = y + jnp.dot(h_ref[0].astype(BF16), m_ref[0], preferred_element_type=F32)
    for r in range(lc):
        rows = pl.ds(r, nk, stride=lc)
        o_ref[rows, :] = y[:, r * LANES:(r + 1) * LANES] + d_ref[...] * u_ref[rows, :]


def _ssm_out(u, tmat, hst, mmat, d_row, lc):
    t = u.shape[0]
    nb, w, _ = tmat.shape
    hw = hst.shape[2]
    return pl.pallas_call(
        functools.partial(_ssm_out_kernel, lc=lc),
        grid=(nb,),
        in_specs=[pl.BlockSpec((t, LANES), lambda i: (0, i)),
                  pl.BlockSpec((1, w, w), lambda i: (i, 0, 0)),
                  pl.BlockSpec((1, t // lc, hw), lambda i: (i, 0, 0)),
                  pl.BlockSpec((1, hw, w), lambda i: (i, 0, 0)),
                  pl.BlockSpec((1, LANES), lambda i: (0, i))],
        out_specs=pl.BlockSpec((t, LANES), lambda i: (0, i)),
        out_shape=jax.ShapeDtypeStruct(u.shape, F32),
        compiler_params=_cparams(("parallel",)),
        name="ssm_out",
    )(u, tmat, hst, mmat, d_row)


def _ssm_prompt(u, coefs, ssm_d):
    tmat, smat, mmat, _, _, apow_re, apow_im, _, _ = coefs
    g, p, lc = N_SSM_GROUPS, SSM_STATE, SCAN_CHUNK
    nb = g // GBLK
    hw = GBLK * p
    s_loc = _ssm_local(u, smat, lc)
    hst, fin = _ssm_rec(s_loc.transpose(1, 0, 2), apow_re.reshape(nb, hw),
                        apow_im.reshape(nb, hw))
    y = _ssm_out(u, tmat, hst.transpose(1, 0, 2), mmat, ssm_d.reshape(1, SSM_WIDTH), lc)
    return y, fin[:, :hw].reshape(g, p), fin[:, hw:].reshape(g, p)


def _ssm_sample_kernel(u_ref, hr_ref, hi_ref, are_ref, aim_ref, bbr_ref, bbi_ref,
                       cr_ref, ci_ref, d_ref, y_ref, fr_ref, fi_ref, *, nb, steps):
    nt = (((1,), (0,)), ((), ()))
    ntt = (((1,), (1,)), ((), ()))
    for gi in range(u_ref.shape[0]):
        u = u_ref[gi]
        bu_re = lax.dot_general(u, bbr_ref[gi], nt, precision=_HI, preferred_element_type=F32)
        bu_im = lax.dot_general(u, bbi_ref[gi], nt, precision=_HI, preferred_element_type=F32)
        a_re, a_im = are_ref[gi], aim_ref[gi]
        hr, hi = hr_ref[gi], hi_ref[gi]
        hs_re, hs_im = [], []
        for l in range(steps):
            sl = slice(l * nb, (l + 1) * nb)
            hr, hi = (a_re * hr - a_im * hi + bu_re[sl], a_re * hi + a_im * hr + bu_im[sl])
            hs_re.append(hr)
            hs_im.append(hi)
        h_re = jnp.concatenate(hs_re, axis=0).astype(BF16)
        h_im = jnp.concatenate(hs_im, axis=0).astype(BF16)
        y = (lax.dot_general(h_re, cr_ref[gi].astype(BF16), ntt, preferred_element_type=F32)
             - lax.dot_general(h_im, ci_ref[gi].astype(BF16), ntt, preferred_element_type=F32))
        y_ref[gi] = y + d_ref[gi] * u
        fr_ref[gi] = hr
        fi_ref[gi] = hi


def _ssm_sample(u_g, h0_re, h0_im, coefs, c_re, c_im, ssm_d, nb, steps):
    g, p, c = N_SSM_GROUPS, SSM_STATE, SSM_GROUP
    abar_re, abar_im = coefs[3].reshape(g, 1, p), coefs[4].reshape(g, 1, p)
    bb_re, bb_im = coefs[7].reshape(g, c, p), coefs[8].reshape(g, c, p)
    rows = steps * nb
    sds = jax.ShapeDtypeStruct
    spec = lambda a, b: pl.BlockSpec((GBLK, a, b), lambda i: (i, 0, 0))
    return pl.pallas_call(
        functools.partial(_ssm_sample_kernel, nb=nb, steps=steps),
        grid=(g // GBLK,),
        in_specs=[spec(rows, c), spec(nb, p), spec(nb, p), spec(1, p), spec(1, p),
                  spec(c, p), spec(c, p), spec(c, p), spec(c, p), spec(1, c)],
        out_specs=[spec(rows, c), spec(nb, p), spec(nb, p)],
        out_shape=[sds((g, rows, c), F32), sds((g, nb, p), F32), sds((g, nb, p), F32)],
        compiler_params=_cparams(("parallel",)),
        name="ssm_sample",
    )(u_g, h0_re, h0_im, abar_re, abar_im, bb_re, bb_im, c_re, c_im,
      ssm_d.reshape(g, 1, c))


def _layer_norm(x, g, b):
    mu = jnp.mean(x, axis=-1, keepdims=True)
    var = jnp.mean(jnp.square(x - mu), axis=-1, keepdims=True)
    return (x - mu) * lax.rsqrt(var + LN_EPS) * g + b


def _merge_kernel(o_ref, y_ref, x_ref, gw_ref, gb_ref, wo_ref, g_ref, b_ref,
                  x1_ref, x1t_ref):
    z = jax.nn.gelu(y_ref[...])
    gate = jnp.dot(z.astype(BF16), gw_ref[...], preferred_element_type=F32) + gb_ref[...]
    z = z * jax.nn.sigmoid(gate)
    mix = jnp.dot(o_ref[...].astype(BF16), wo_ref[0:ATTN_WIDTH, :], preferred_element_type=F32)
    mix = mix + jnp.dot(z.astype(BF16), wo_ref[ATTN_WIDTH:, :], preferred_element_type=F32)
    x1 = _layer_norm(DEEPNORM_ALPHA * x_ref[...] + mix, g_ref[...], b_ref[...])
    x1_ref[...] = x1
    x1t_ref[...] = x1.T.astype(BF16)


def _merge(attn_o, y_ssm, x, glu_w, glu_b, w_out, ln_g, ln_b, tm):
    t = x.shape[0]
    full = lambda r, c: pl.BlockSpec((r, c), lambda i: (0, 0), pipeline_mode=pl.Buffered(1))
    return pl.pallas_call(
        _merge_kernel,
        grid=(t // tm,),
        in_specs=[pl.BlockSpec((tm, ATTN_WIDTH), lambda i: (i, 0)),
                  pl.BlockSpec((tm, SSM_WIDTH), lambda i: (i, 0)),
                  pl.BlockSpec((tm, D_MODEL), lambda i: (i, 0)),
                  full(SSM_WIDTH, SSM_WIDTH), full(1, SSM_WIDTH),
                  full(D_MODEL, D_MODEL), full(1, D_MODEL), full(1, D_MODEL)],
        out_specs=[pl.BlockSpec((tm, D_MODEL), lambda i: (i, 0)),
                   pl.BlockSpec((D_MODEL, tm), lambda i: (0, i))],
        out_shape=[jax.ShapeDtypeStruct((t, D_MODEL), F32),
                   jax.ShapeDtypeStruct((D_MODEL, t), BF16)],
        compiler_params=_cparams(("parallel",)),
        name="merge_out_ln1",
    )(attn_o, y_ssm, x, glu_w, glu_b, w_out, ln_g, ln_b)


def _top_desc(vals, n):
    rows = []
    for r in range(n):
        mx = jnp.max(vals, axis=0, keepdims=True)
        rows.append(mx)
        if r + 1 < n:
            vals = jnp.where(vals == mx, -jnp.inf, vals)
    return rows


def _route_kernel(xt_ref, wq_ref, keys_ref, e1_ref, e2_ref, te_ref):
    k = PEER_TOPK
    qt = jnp.dot(wq_ref[...], xt_ref[...], preferred_element_type=F32).astype(BF16)
    for h in range(PEER_HEADS):
        tops = []
        for m in range(2):
            hm = 2 * h + m
            sc = jnp.dot(keys_ref[hm], qt[hm * PEER_NKEYS:(hm + 1) * PEER_NKEYS, :],
                         preferred_element_type=F32)
            tops.append((sc, _top_desc(sc, k + 1)))
        (sc1, a), (sc2, b) = tops
        ninf = jnp.full_like(a[0], -jnp.inf)
        pad = [ninf] * (-(k + 1) % 8)
        amat = jnp.concatenate(a + pad, axis=0)
        bmat = jnp.concatenate(b + pad, axis=0)
        row8 = lax.broadcasted_iota(jnp.int32, (8, amat.shape[1]), 0)
        groups = [a[0] + bmat]
        for r1 in range(2, 9):
            groups.append(jnp.where(row8 < (k + 1) // r1, a[r1 - 1] + bmat[0:8], -jnp.inf))
        groups.append(amat[8:] + b[0])
        cand = jnp.concatenate(groups, axis=0)
        best = _top_desc(cand, k + 1)
        zsum = jnp.ones_like(best[0])
        for r in range(1, k):
            zsum = zsum + jnp.exp(best[r] - best[0])
        theta = 0.5 * (best[k - 1] + best[k])
        e1_ref[h] = jnp.exp(sc1 - a[0]) / zsum
        e2_ref[h] = jnp.exp(sc2 - b[0])
        te_ref[h] = jnp.exp((theta - b[0]) - sc1)


def _route(x1t, wq_t, keys, tm):
    t = x1t.shape[1]
    big = pl.BlockSpec((PEER_HEADS, PEER_NKEYS, tm), lambda i: (0, 0, i))
    sds = jax.ShapeDtypeStruct((PEER_HEADS, PEER_NKEYS, t), F32)
    return pl.pallas_call(
        _route_kernel,
        grid=(t // tm,),
        in_specs=[pl.BlockSpec((D_MODEL, tm), lambda i: (0, i)),
                  pl.BlockSpec(wq_t.shape, lambda i: (0, 0)),
                  pl.BlockSpec(keys.shape, lambda i: (0, 0, 0))],
        out_specs=[big, big, big],
        out_shape=[sds, sds, sds],
        compiler_params=_cparams(("parallel",)),
        name="peer_route",
    )(x1t, wq_t, keys)


IPB = 4
UNIT_I = 2
ATTN_ROW_CHUNK = 256


def _peer_kernel(xt_ref, u_ref, vt_ref, e1_ref, e2_ref, te_ref, o_ref, *, tsub):
    eb = pl.program_id(1)

    @pl.when(eb == 0)
    def _zero():
        o_ref[...] = jnp.zeros(o_ref.shape, F32)

    n_sub = xt_ref.shape[1] // tsub
    upb = IPB // UNIT_I
    units = [(j, un) for j in range(n_sub) for un in range(upb)]
    urows = UNIT_I * PEER_NKEYS
    vrows = D_MODEL // upb

    def up(j, un):
        return jnp.dot(u_ref[un * urows:(un + 1) * urows, :],
                       xt_ref[:, j * tsub:(j + 1) * tsub], preferred_element_type=F32)

    def down_piece(j, a, piece):
        rows = slice(piece * vrows, (piece + 1) * vrows)
        o_ref[rows, j * tsub:(j + 1) * tsub] += jnp.dot(vt_ref[rows, :], a,
                                                        preferred_element_type=F32)

    def gate(j, ii, ht):
        i = eb * IPB + ii
        cols = slice(j * tsub, (j + 1) * tsub)
        e1rows = [e1_ref[h, pl.ds(i, 1), cols] for h in range(PEER_HEADS)]
        terows = [te_ref[h, pl.ds(i, 1), cols] for h in range(PEER_HEADS)]
        a_cols = []
        for c0 in range(0, tsub, LANES):
            hl = slice(c0, c0 + LANES)
            lc = slice(j * tsub + c0, j * tsub + c0 + LANES)
            w = None
            for h in range(PEER_HEADS):
                e2 = e2_ref[h, :, lc]
                wh = jnp.where(e2 >= terows[h][:, hl], e2, 0.0) * e1rows[h][:, hl]
                w = wh if w is None else w + wh
            a_cols.append((w * jax.nn.gelu(ht[:, hl])).astype(BF16))
        return a_cols[0] if len(a_cols) == 1 else jnp.concatenate(a_cols, axis=1)

    ht_next = up(*units[0])
    a_prev, a_rows = None, []
    for n, (j, un) in enumerate(units):
        ht = ht_next
        if n + 1 < len(units):
            ht_next = up(*units[n + 1])
        if a_prev is not None:
            down_piece(j - 1, a_prev, un)
        for k in range(UNIT_I):
            a_rows.append(gate(j, un * UNIT_I + k, ht[k * PEER_NKEYS:(k + 1) * PEER_NKEYS, :]))
        if un == upb - 1:
            a_prev, a_rows = jnp.concatenate(a_rows, axis=0), []
    for piece in range(upb):
        down_piece(n_sub - 1, a_prev, piece)


def _peer(x1t, u_tab, vt_tab, e1, e2, te_thr, tm, tsub):
    t = x1t.shape[1]
    te = IPB * PEER_NKEYS
    n_eb = u_tab.shape[0] // te
    once = pl.Buffered(1)
    big = pl.BlockSpec((PEER_HEADS, PEER_NKEYS, tm), lambda i, e: (0, 0, i), pipeline_mode=once)
    return pl.pallas_call(
        functools.partial(_peer_kernel, tsub=tsub),
        grid=(t // tm, n_eb),
        in_specs=[pl.BlockSpec((D_MODEL, tm), lambda i, e: (0, i), pipeline_mode=once),
                  pl.BlockSpec((te, D_MODEL), lambda i, e: (e, 0)),
                  pl.BlockSpec((D_MODEL, te), lambda i, e: (0, e)),
                  big, big, big],
        out_specs=pl.BlockSpec((D_MODEL, tm), lambda i, e: (0, i)),
        out_shape=jax.ShapeDtypeStruct((D_MODEL, t), F32),
        compiler_params=_cparams(("parallel", "arbitrary")),
        name="peer_experts",
    )(x1t, u_tab, vt_tab, e1, e2, te_thr)


def _ln2_kernel(x1_ref, ft_ref, g_ref, b_ref, o_ref):
    o_ref[...] = _layer_norm(DEEPNORM_ALPHA * x1_ref[...] + ft_ref[...].T,
                             g_ref[...], b_ref[...])


def _ln2(x1, ffn_t, ln_g, ln_b, tm):
    t = x1.shape[0]
    return pl.pallas_call(
        _ln2_kernel,
        grid=(t // tm,),
        in_specs=[pl.BlockSpec((tm, D_MODEL), lambda i: (i, 0)),
                  pl.BlockSpec((D_MODEL, tm), lambda i: (0, i)),
                  pl.BlockSpec((1, D_MODEL), lambda i: (0, 0)),
                  pl.BlockSpec((1, D_MODEL), lambda i: (0, 0))],
        out_specs=pl.BlockSpec((tm, D_MODEL), lambda i: (i, 0)),
        out_shape=jax.ShapeDtypeStruct((t, D_MODEL), F32),
        compiler_params=_cparams(("parallel",)),
        name="ln2",
    )(x1, ffn_t, ln_g, ln_b)


def _sample_buckets(past_len, dec_seq):
    qpos = past_len + np.tile(np.arange(dec_seq), 2)[:, None]
    w = PPS * PAGE_SIZE
    kpos = (past_len - w) + np.arange(w)[None, :]
    last = _bucket_np(qpos - kpos)
    jn = np.arange(NEWK)[None, :]
    knew = past_len + jn
    new = np.where((jn < dec_seq) & (knew <= qpos), _bucket_np(qpos - knew), -1)
    return last[None].astype(np.int32), new[None].astype(np.int32)


def _group_tail(x1, x1t, w_query_t, keys, u_tab, vt_tab, ln_g, ln_b, tm_route, tm_peer, tm_ln):
    e1, e2, te_thr = _route(x1t, w_query_t, keys, tm_route)
    ffn_t = _peer(x1t, u_tab, vt_tab, e1, e2, te_thr, tm_peer, min(tm_peer, 256))
    return _ln2(x1, ffn_t, ln_g, ln_b, tm_ln)


def kernel(x_prompt, x_sample, cache_k, cache_v, state_ssm_re, state_ssm_im, page_table, w_in, lambda_q1, lambda_k1, lambda_q2, lambda_k2, attn_subln_g, rel_bias, ssm_a_re, ssm_a_im, ssm_b_re, ssm_b_im, ssm_c_re, ssm_c_im, ssm_d, ssm_log_dt, ssm_glu_w, ssm_glu_b, w_out, ln1_g, ln1_b, peer_w_query, peer_sub_keys, peer_u, peer_v, ln2_g, ln2_b):
    l = 0
    seq = x_prompt.shape[1]
    dec_b, dec_s = x_sample.shape[0], x_sample.shape[1]
    n_pages = page_table.shape[1]
    past_len = n_pages * PAGE_SIZE
    g, p, c = N_SSM_GROUPS, SSM_STATE, SSM_GROUP
    lam_init = 0.8 - 0.6 * math.exp(-0.3 * l)
    lam = (jnp.exp(jnp.sum(lambda_q1[l] * lambda_k1[l]))
           - jnp.exp(jnp.sum(lambda_q2[l] * lambda_k2[l])) + lam_init).reshape(1).astype(F32)

    w_in_b = w_in[l].astype(BF16)
    glu_w_b = ssm_glu_w[l].astype(BF16)
    w_out_b = w_out[l].astype(BF16)
    wq_t = peer_w_query[l].T.astype(BF16)
    keys = peer_sub_keys[l].reshape(2 * PEER_HEADS, PEER_NKEYS, PEER_NKEYS).astype(BF16)
    u_tab = peer_u[l].astype(BF16)
    vt_tab = peer_v[l].T.astype(BF16)
    subln_g = attn_subln_g[l].reshape(1, HEAD_DIM)
    glu_b = ssm_glu_b[l].reshape(1, SSM_WIDTH)
    g1, b1 = ln1_g[l].reshape(1, D_MODEL), ln1_b[l].reshape(1, D_MODEL)
    g2, b2 = ln2_g[l].reshape(1, D_MODEL), ln2_b[l].reshape(1, D_MODEL)
    coefs = _ssm_coefs(ssm_log_dt[l], ssm_a_re[l], ssm_a_im[l],
                       ssm_b_re[l].transpose(0, 2, 1), ssm_b_im[l].transpose(0, 2, 1),
                       ssm_c_re[l], ssm_c_im[l])

    xp = x_prompt.reshape(seq, D_MODEL)
    tb = 1024
    q_p, k_p, v_p, u_p = _proj(xp, w_in_b, 512, ATTN_WIDTH)
    bias_p = _band_bias(rel_bias, tb, LOG2E)
    attn_p = _attn_prompt(q_p, k_p, v_p, bias_p, lam, subln_g, tb, lam_init)
    y_p, hr_p, hi_p = _ssm_prompt(u_p, coefs, ssm_d[l])
    x1_p, x1t_p = _merge(attn_p, y_p, xp, glu_w_b, glu_b, w_out_b, g1, b1, 512)
    out_p = _group_tail(x1_p, x1t_p, wq_t, keys, u_tab, vt_tab, g2, b2, 256, 1024, 256)

    ts = dec_b * dec_s
    xs = x_sample.reshape(ts, D_MODEL)
    qkvu_s = _proj(xs, w_in_b, ts, ATTN_WIDTH)
    q_s = qkvu_s[0].reshape(dec_b, dec_s, N_HEADS, HEAD_DIM) * (HALF_DIM ** -0.5)
    lane = np.arange(HEAD_DIM)
    halves = jnp.asarray(np.stack([lane < HALF_DIM, lane >= HALF_DIM]).astype(np.float32))
    q2 = (q_s[:, None] * halves[None, :, None, None, :])
    q2 = q2.transpose(0, 3, 1, 2, 4).reshape(dec_b, N_HEADS, QROWS, HEAD_DIM)
    pad_new = lambda a: jnp.pad(
        a.reshape(dec_b, dec_s, N_HEADS, HEAD_DIM).transpose(0, 2, 1, 3),
        ((0, 0), (0, 0), (0, NEWK - dec_s), (0, 0)))
    bl, bn = _sample_buckets(past_len, dec_s)
    bias_last = _bias_tiles(rel_bias, bl)[:, 0]
    bias_new = _bias_tiles(rel_bias, bn)[:, 0]
    attn_s = _attn_sample(q2, pad_new(qkvu_s[1]), pad_new(qkvu_s[2]), cache_k[l], cache_v[l],
                          page_table, bias_last, bias_new, lam, subln_g, lam_init)
    u_s = qkvu_s[3].reshape(dec_b, dec_s, g, c).transpose(2, 1, 0, 3).reshape(g, ts, c)
    y_g, hr_s, hi_s = _ssm_sample(u_s, state_ssm_re[l].transpose(1, 0, 2),
                                  state_ssm_im[l].transpose(1, 0, 2), coefs,
                                  ssm_c_re[l], ssm_c_im[l], ssm_d[l], dec_b, dec_s)
    y_s = y_g.reshape(g, dec_s, dec_b, c).transpose(2, 1, 0, 3).reshape(ts, SSM_WIDTH)
    x1_s, x1t_s = _merge(attn_s.reshape(ts, ATTN_WIDTH), y_s, xs, glu_w_b, glu_b, w_out_b,
                         g1, b1, ts)
    out_s = _group_tail(x1_s, x1t_s, wq_t, keys, u_tab, vt_tab, g2, b2, ts, ts, ts)

    kv = lambda a, b_, s_: a.reshape(1, b_, s_, N_HEADS, HEAD_DIM)
    return (out_p.reshape(1, seq, D_MODEL), out_s.reshape(dec_b, dec_s, D_MODEL),
            kv(k_p, 1, seq), kv(v_p, 1, seq),
            hr_p.reshape(1, 1, g, p), hi_p.reshape(1, 1, g, p),
            kv(qkvu_s[1], dec_b, dec_s), kv(qkvu_s[2], dec_b, dec_s),
            hr_s.transpose(1, 0, 2)[None], hi_s.transpose(1, 0, 2)[None])
```

```python
import functools
import math

import numpy as np
import jax
import jax.numpy as jnp
from jax import lax
from jax.experimental import pallas as pl
from jax.experimental.pallas import tpu as pltpu

F32 = jnp.float32
BF16 = jnp.bfloat16

D_MODEL = 2048
PAGE_SIZE = 128
ATTN_WIDTH = 1024
SSM_WIDTH = 1024
N_HEADS = 8
HEAD_DIM = 128
HALF_DIM = 64
N_BUCKETS = 32
MAX_DISTANCE = 128
SSM_GROUP = 16
N_SSM_GROUPS = 64
SSM_STATE = 64
PEER_HEADS = 8
PEER_NKEYS = 128
PEER_TOPK = 16
LN_EPS = 1e-5
DEPTH = 1
DEEPNORM_ALPHA = (2.0 * DEPTH) ** 0.25

LANES = 128
SCAN_CHUNK = 8
GBLK = LANES // SSM_GROUP
NEG = -1e30
LOG2E = 1.4426950408889634
VMEM_LIMIT = 56 * 1024 * 1024

_HI = lax.Precision.HIGHEST


def _cparams(sem):
    return pltpu.CompilerParams(dimension_semantics=sem, vmem_limit_bytes=VMEM_LIMIT)


def _rep(x, n):
    return x if n == 1 else jnp.concatenate([x] * n, axis=1)


def _proj_kernel(x_ref, w_ref, *o_refs):
    xb = x_ref[...].astype(BF16)
    tn = o_refs[0].shape[1]
    for j, o_ref in enumerate(o_refs):
        o_ref[...] = jnp.dot(xb, w_ref[:, j * tn:(j + 1) * tn], preferred_element_type=F32)


def _proj(x, w, tm, tn):
    t, k = x.shape
    n = w.shape[1]
    return pl.pallas_call(
        _proj_kernel,
        grid=(t // tm,),
        in_specs=[pl.BlockSpec((tm, k), lambda i: (i, 0)),
                  pl.BlockSpec((k, n), lambda i: (0, 0), pipeline_mode=pl.Buffered(1))],
        out_specs=[pl.BlockSpec((tm, tn), lambda i: (i, 0))] * (n // tn),
        out_shape=[jax.ShapeDtypeStruct((t, tn), F32)] * (n // tn),
        compiler_params=_cparams(("parallel",)),
        name="in_proj",
    )(x, w)


def _bucket_np(d):
    n = np.maximum(d, 0)
    max_exact = N_BUCKETS // 2
    nf = np.maximum(n, 1).astype(np.float64)
    large = max_exact + (np.log(nf / max_exact) / math.log(MAX_DISTANCE / max_exact)
                         * (N_BUCKETS - max_exact)).astype(np.int32)
    large = np.minimum(large, N_BUCKETS - 1)
    return np.where(n < max_exact, n, large).astype(np.int32)


def _bias_kernel(rb_ref, bkt_ref, o_ref, *, scale):
    h = pl.program_id(0)
    bkt = bkt_ref[0]
    far = rb_ref[N_BUCKETS - 1, h]
    acc = jnp.where(bkt < 0, NEG, 0.0).astype(F32)
    for b in range(N_BUCKETS - 1):
        acc = jnp.where(bkt == b, (rb_ref[b, h] - far) * scale, acc)
    o_ref[0, 0] = acc


def _bias_tiles(rel_bias, bkt, scale=1.0):
    n, r, c = bkt.shape
    return pl.pallas_call(
        functools.partial(_bias_kernel, scale=scale),
        grid=(N_HEADS, n),
        in_specs=[pl.BlockSpec(memory_space=pltpu.SMEM),
                  pl.BlockSpec((1, r, c), lambda h, i: (i, 0, 0))],
        out_specs=pl.BlockSpec((1, 1, r, c), lambda h, i: (h, i, 0, 0)),
        out_shape=jax.ShapeDtypeStruct((N_HEADS, n, r, c), F32),
        compiler_params=_cparams(("parallel", "arbitrary")),
        name="bias_tiles",
    )(rel_bias, jnp.asarray(bkt))


def _band_bias_kernel(rb_ref, bkt_ref, o_ref, *, tb, scale):
    h = pl.program_id(0)
    far = rb_ref[N_BUCKETS - 1, h]

    def block(bkt):
        acc = jnp.where(bkt < 0, NEG, 0.0).astype(F32)
        for b in range(N_BUCKETS - 1):
            acc = jnp.where(bkt == b, (rb_ref[b, h] - far) * scale, acc)
        return acc

    on_diag, below = block(bkt_ref[0]), block(bkt_ref[1])
    zero = jnp.zeros((LANES, LANES), F32)
    masked = jnp.full((LANES, LANES), NEG, F32)
    nblk = tb // LANES
    for a in range(nblk):
        row = [on_diag if b == a else below if b == a - 1 else zero if b < a else masked
               for b in range(nblk)]
        o_ref[0, 0, a * LANES:(a + 1) * LANES, :] = jnp.concatenate(row, axis=1)
    o_ref[0, 1] = jnp.zeros((tb, tb), F32)
    o_ref[0, 1, 0:LANES, tb - LANES:tb] = below


def _band_bias(rel_bias, tb, scale):
    assert MAX_DISTANCE <= LANES and tb % LANES == 0
    r = np.arange(LANES)[:, None]
    c = np.arange(LANES)[None, :]
    bkt = np.stack([np.where(c <= r, _bucket_np(r - c), -1), _bucket_np(r - c + LANES)])
    return pl.pallas_call(
        functools.partial(_band_bias_kernel, tb=tb, scale=scale),
        grid=(N_HEADS,),
        in_specs=[pl.BlockSpec(memory_space=pltpu.SMEM),
                  pl.BlockSpec((2, LANES, LANES), lambda h: (0, 0, 0))],
        out_specs=pl.BlockSpec((1, 2, tb, tb), lambda h: (h, 0, 0, 0)),
        out_shape=jax.ShapeDtypeStruct((N_HEADS, 2, tb, tb), F32),
        compiler_params=_cparams(("parallel",)),
        name="band_bias",
    )(rel_bias, jnp.asarray(bkt.astype(np.int32)))


def _subln(attn, g_row, lam_init):
    ms = jnp.mean(jnp.square(attn), axis=-1, keepdims=True)
    return attn * lax.rsqrt(ms + LN_EPS) * g_row * (1.0 - lam_init)


def _attn_kernel(qi_ref, ki_ref, lam_ref, q_ref, k_ref, v_ref, bias_ref, g_ref, o_ref,
                 q2_sc, m_sc, l_sc, acc_sc, *, tb, lam_init):
    s = pl.program_id(1)
    qi = qi_ref[s]
    ki = ki_ref[s]

    @pl.when(ki == 0)
    def _init():
        q = q_ref[...] * (HALF_DIM ** -0.5 * LOG2E)
        lane = lax.broadcasted_iota(jnp.int32, q.shape, 1)
        q2_sc[0:tb, :] = jnp.where(lane < HALF_DIM, q, 0.0).astype(BF16)
        q2_sc[tb:2 * tb, :] = jnp.where(lane >= HALF_DIM, q, 0.0).astype(BF16)
        m_sc[...] = jnp.full(m_sc.shape, NEG, F32)
        l_sc[...] = jnp.zeros(l_sc.shape, F32)
        acc_sc[...] = jnp.zeros(acc_sc.shape, F32)

    nt = (((1,), (1,)), ((), ()))
    rc = min(tb, ATTN_ROW_CHUNK)

    def update(bias_tile, causal):
        kb = k_ref[...].astype(BF16)
        vb = v_ref[...].astype(BF16)
        n_chunks = 2 * tb // rc

        def width(c):
            return (c * rc) % tb + rc if causal else tb

        def qk(c):
            w = width(c)
            sc = lax.dot_general(q2_sc[c * rc:(c + 1) * rc, :], kb[0:w], nt,
                                 preferred_element_type=F32)
            if bias_tile is not None:
                r0 = (c * rc) % tb
                sc = sc + bias_ref[0, bias_tile, r0:r0 + rc, 0:w]
            return sc

        def softmax(c, sc):
            rows = slice(c * rc, (c + 1) * rc)
            m_prev = m_sc[rows]
            m_new = jnp.maximum(m_prev, jnp.max(sc, axis=-1, keepdims=True))
            alpha = jnp.exp2(m_prev - m_new)
            p = jnp.exp2(sc - _rep(m_new, sc.shape[1] // LANES))
            l_sc[rows] = alpha * l_sc[rows] + jnp.sum(p, axis=-1, keepdims=True)
            m_sc[rows] = m_new
            return p.astype(BF16), alpha

        def pv(c, p, alpha):
            rows = slice(c * rc, (c + 1) * rc)
            acc_sc[rows] = alpha * acc_sc[rows] + jnp.dot(p, vb[0:width(c)],
                                                          preferred_element_type=F32)

        sc_next = qk(0)
        for c in range(n_chunks):
            sc = sc_next
            if c + 1 < n_chunks:
                sc_next = qk(c + 1)
            p, alpha = softmax(c, sc)
            pv(c, p, alpha)

    @pl.when(ki < qi - 1)
    def _far():
        update(None, False)

    @pl.when(ki == qi - 1)
    def _sub():
        update(1, False)

    @pl.when(ki == qi)
    def _diag():
        update(0, True)
        o = acc_sc[...] / l_sc[...]
        attn = o[0:tb] - lam_ref[0] * o[tb:2 * tb]
        o_ref[...] = _subln(attn, g_ref[...], lam_init)


def _attn_prompt(q, k, v, bias, lam, subln_g, tb, lam_init):
    t = q.shape[0]
    nb = t // tb
    qi_idx = np.concatenate([np.full(i + 1, i) for i in range(nb)]).astype(np.int32)
    ki_idx = np.concatenate([np.arange(i + 1) for i in range(nb)]).astype(np.int32)
    grid_spec = pltpu.PrefetchScalarGridSpec(
        num_scalar_prefetch=2,
        grid=(N_HEADS, len(qi_idx)),
        in_specs=[
            pl.BlockSpec(memory_space=pltpu.SMEM),
            pl.BlockSpec((tb, HEAD_DIM), lambda h, s, qi, ki: (qi[s], h)),
            pl.BlockSpec((tb, HEAD_DIM), lambda h, s, qi, ki: (ki[s], h)),
            pl.BlockSpec((tb, HEAD_DIM), lambda h, s, qi, ki: (ki[s], h)),
            pl.BlockSpec((1, 2, tb, tb), lambda h, s, qi, ki: (h, 0, 0, 0)),
            pl.BlockSpec((1, HEAD_DIM), lambda h, s, qi, ki: (0, 0)),
        ],
        out_specs=pl.BlockSpec((tb, HEAD_DIM), lambda h, s, qi, ki: (qi[s], h)),
        scratch_shapes=[
            pltpu.VMEM((2 * tb, HEAD_DIM), BF16),
            pltpu.VMEM((2 * tb, LANES), F32),
            pltpu.VMEM((2 * tb, LANES), F32),
            pltpu.VMEM((2 * tb, HEAD_DIM), F32),
        ],
    )
    return pl.pallas_call(
        functools.partial(_attn_kernel, tb=tb, lam_init=lam_init),
        grid_spec=grid_spec,
        out_shape=jax.ShapeDtypeStruct((t, ATTN_WIDTH), F32),
        compiler_params=_cparams(("parallel", "arbitrary")),
        name="attn_prompt",
    )(jnp.asarray(qi_idx), jnp.asarray(ki_idx), lam, q, k, v, bias, subln_g)


PPS = 16
QROWS = 8
NEWK = 16


def _decode_kernel(pt_ref, lam_ref, q2_ref, kn_ref, vn_ref, bl_ref, bn_ref, g_ref, *rest,
                   n_steps, lam_init):
    k_refs = rest[0:PPS]
    v_refs = rest[PPS:2 * PPS]
    o_ref = rest[2 * PPS]
    m_sc, l_sc, acc_sc = rest[2 * PPS + 1:]
    p_idx = pl.program_id(1)
    last = p_idx == n_steps - 1

    @pl.when(p_idx == 0)
    def _init():
        m_sc[...] = jnp.full(m_sc.shape, NEG, F32)
        l_sc[...] = jnp.zeros(l_sc.shape, F32)
        acc_sc[...] = jnp.zeros(acc_sc.shape, F32)

    last_f = jnp.where(last, 1.0, 0.0).astype(F32)

    def softmax_step(h, sc):
        m_prev = m_sc[h]
        m_new = jnp.maximum(m_prev, jnp.max(sc, axis=-1, keepdims=True))
        alpha = jnp.exp(m_prev - m_new)
        p = jnp.exp(sc - _rep(m_new, sc.shape[1] // LANES) if sc.shape[1] >= LANES
                    else sc - m_new[:, 0:sc.shape[1]])
        l_sc[h] = alpha * l_sc[h] + jnp.sum(p, axis=-1, keepdims=True)
        m_sc[h] = m_new
        return p.astype(BF16), alpha

    def accumulate(h, p, alpha, vmat):
        acc_sc[h] = alpha * acc_sc[h] + jnp.dot(p, vmat, preferred_element_type=F32)

    def update(h, sc, vmat):
        p, alpha = softmax_step(h, sc)
        accumulate(h, p, alpha, vmat)

    def head_rows(refs, h):
        rows_h = pl.ds(h, PAGE_SIZE, stride=N_HEADS)
        return jnp.concatenate([r[0, rows_h, :] for r in refs], axis=0).astype(BF16)

    scores = []
    for h in range(N_HEADS):
        q2 = q2_ref[0, h].astype(BF16)
        sc = lax.dot_general(q2, head_rows(k_refs, h), (((1,), (1,)), ((), ())),
                             preferred_element_type=F32)
        scores.append(sc + bl_ref[h] * last_f)
    probs = [softmax_step(h, scores[h]) for h in range(N_HEADS)]
    for h in range(N_HEADS):
        accumulate(h, probs[h][0], probs[h][1], head_rows(v_refs, h))

    @pl.when(last)
    def _finish():
        for h in range(N_HEADS):
            q2 = q2_ref[0, h].astype(BF16)
            kn = kn_ref[0, h].astype(BF16)
            vn = vn_ref[0, h].astype(BF16)
            sc = lax.dot_general(q2, kn, (((1,), (1,)), ((), ())), preferred_element_type=F32)
            update(h, sc + bn_ref[h], vn)
            o = acc_sc[h] / l_sc[h]
            half = QROWS // 2
            attn = o[0:half] - lam_ref[0] * o[half:QROWS]
            o_ref[0, :, h * HEAD_DIM:(h + 1) * HEAD_DIM] = _subln(attn, g_ref[...], lam_init)


def _attn_sample(q2, k_new, v_new, cache_k, cache_v, page_table, bias_last, bias_new,
                 lam, subln_g, lam_init):
    nb, n_pages = page_table.shape
    n_steps = n_pages // PPS
    dec_seq = QROWS // 2
    pages = lambda cache: cache.reshape(cache.shape[0], PAGE_SIZE * N_HEADS, HEAD_DIM)

    def page_spec(j):
        return pl.BlockSpec(
            (1, PAGE_SIZE * N_HEADS, HEAD_DIM),
            lambda b, p, pt, j=j: (pt[b * n_pages + p * PPS + j], 0, 0))

    grid_spec = pltpu.PrefetchScalarGridSpec(
        num_scalar_prefetch=1,
        grid=(nb, n_steps),
        in_specs=[
            pl.BlockSpec(memory_space=pltpu.SMEM),
            pl.BlockSpec((1, N_HEADS, QROWS, HEAD_DIM), lambda b, p, pt: (b, 0, 0, 0)),
            pl.BlockSpec((1, N_HEADS, NEWK, HEAD_DIM), lambda b, p, pt: (b, 0, 0, 0)),
            pl.BlockSpec((1, N_HEADS, NEWK, HEAD_DIM), lambda b, p, pt: (b, 0, 0, 0)),
            pl.BlockSpec((N_HEADS, QROWS, PPS * PAGE_SIZE), lambda b, p, pt: (0, 0, 0)),
            pl.BlockSpec((N_HEADS, QROWS, NEWK), lambda b, p, pt: (0, 0, 0)),
            pl.BlockSpec((1, HEAD_DIM), lambda b, p, pt: (0, 0)),
        ] + [page_spec(j) for j in range(PPS)] + [page_spec(j) for j in range(PPS)],
        out_specs=pl.BlockSpec((1, dec_seq, ATTN_WIDTH), lambda b, p, pt: (b, 0, 0)),
        scratch_shapes=[
            pltpu.VMEM((N_HEADS, QROWS, LANES), F32),
            pltpu.VMEM((N_HEADS, QROWS, LANES), F32),
            pltpu.VMEM((N_HEADS, QROWS, HEAD_DIM), F32),
        ],
    )
    return pl.pallas_call(
        functools.partial(_decode_kernel, n_steps=n_steps, lam_init=lam_init),
        grid_spec=grid_spec,
        out_shape=jax.ShapeDtypeStruct((nb, dec_seq, ATTN_WIDTH), F32),
        compiler_params=_cparams(("parallel", "arbitrary")),
        name="attn_sample",
    )(page_table.reshape(-1), lam, q2, k_new, v_new, bias_last, bias_new, subln_g,
      *([pages(cache_k)] * PPS), *([pages(cache_v)] * PPS))


def _ssm_coef_kernel(ldt_ref, ar_ref, ai_ref, bt_re_ref, bt_im_ref, c_re_ref, c_im_ref,
                     t_ref, s_ref, m_ref,
                     abar_re_ref, abar_im_ref, apow_re_ref, apow_im_ref,
                     bb_re_ref, bb_im_ref):
    lc, c, p = SCAN_CHUNK, SSM_GROUP, SSM_STATE
    hw = GBLK * p
    dt = jnp.exp(ldt_ref[0])
    ar, ai = ar_ref[0], ai_ref[0]
    mag = jnp.exp(ar * dt)
    abar_re = mag * jnp.cos(ai * dt)
    abar_im = mag * jnp.sin(ai * dt)
    nr, ni = abar_re - 1.0, abar_im
    den = ar * ar + ai * ai
    f_re = ((nr * ar + ni * ai) / den)[:, None, :]
    f_im = ((ni * ar - nr * ai) / den)[:, None, :]
    bt_re, bt_im = bt_re_ref[0], bt_im_ref[0]
    bb_re = f_re * bt_re - f_im * bt_im
    bb_im = f_re * bt_im + f_im * bt_re
    c_re, c_im = c_re_ref[0], c_im_ref[0]

    pw = [(jnp.ones_like(abar_re), jnp.zeros_like(abar_re))]
    for _ in range(lc):
        pr, pi = pw[-1]
        pw.append((pr * abar_re - pi * abar_im, pr * abar_im + pi * abar_re))

    rid = lax.broadcasted_iota(jnp.int32, (LANES, hw), 0) // c
    cid = lax.broadcasted_iota(jnp.int32, (LANES, hw), 1) // p
    same_group = rid == cid

    def bdiag(x):
        flat = x.reshape(LANES, p)
        return jnp.where(same_group, jnp.concatenate([flat] * GBLK, axis=1), 0.0)

    def cmul(xr, xi, w):
        wr, wi = w[0][:, None, :], w[1][:, None, :]
        return xr * wr - xi * wi, xr * wi + xi * wr

    l_re, l_im = bdiag(bb_re), bdiag(bb_im)
    nt = (((1,), (1,)), ((), ()))
    zero_tile = jnp.zeros((LANES, LANES), BF16)
    bd = []
    for tau in range(lc + 1):
        cr, ci = cmul(c_re, c_im, pw[tau])
        r_re, r_im = bdiag(cr), bdiag(ci)
        if tau < lc:
            bd.append((lax.dot_general(l_re, r_re, nt, precision=_HI, preferred_element_type=F32)
                       - lax.dot_general(l_im, r_im, nt, precision=_HI,
                                         preferred_element_type=F32)).astype(BF16))
        if tau > 0:
            cols = slice((tau - 1) * LANES, tau * LANES)
            m_ref[0, 0:hw, cols] = r_re.T.astype(BF16)
            m_ref[0, hw:2 * hw, cols] = (-r_im).T.astype(BF16)
    for s in range(lc):
        rows = slice(s * LANES, (s + 1) * LANES)
        for r in range(lc):
            t_ref[0, rows, r * LANES:(r + 1) * LANES] = bd[r - s] if r >= s else zero_tile
        wr, wi = cmul(bb_re, bb_im, pw[lc - 1 - s])
        s_ref[0, rows, 0:hw] = bdiag(wr).astype(BF16)
        s_ref[0, rows, hw:2 * hw] = bdiag(wi).astype(BF16)
    abar_re_ref[0] = abar_re
    abar_im_ref[0] = abar_im
    apow_re_ref[0] = pw[lc][0]
    apow_im_ref[0] = pw[lc][1]
    bb_re_ref[0] = bb_re
    bb_im_ref[0] = bb_im


def _ssm_coefs(log_dt, a_re, a_im, bt_re, bt_im, c_re, c_im):
    g, p, c, lc = N_SSM_GROUPS, SSM_STATE, SSM_GROUP, SCAN_CHUNK
    nb, hw, w = g // GBLK, GBLK * p, lc * LANES
    vec = pl.BlockSpec((1, GBLK, p), lambda i: (i, 0, 0))
    mat = pl.BlockSpec((1, GBLK, c, p), lambda i: (i, 0, 0, 0))
    sds = jax.ShapeDtypeStruct
    blk = lambda a: a.reshape((nb, GBLK) + a.shape[1:])
    return pl.pallas_call(
        _ssm_coef_kernel,
        grid=(nb,),
        in_specs=[pl.BlockSpec((1, GBLK, 1), lambda i: (i, 0, 0)), vec, vec, mat, mat, mat, mat],
        out_specs=[pl.BlockSpec((1, w, w), lambda i: (i, 0, 0)),
                   pl.BlockSpec((1, w, 2 * hw), lambda i: (i, 0, 0)),
                   pl.BlockSpec((1, 2 * hw, w), lambda i: (i, 0, 0)),
                   vec, vec, vec, vec, mat, mat],
        out_shape=[sds((nb, w, w), BF16), sds((nb, w, 2 * hw), BF16), sds((nb, 2 * hw, w), BF16)]
        + [sds((nb, GBLK, p), F32)] * 4 + [sds((nb, GBLK, c, p), F32)] * 2,
        compiler_params=_cparams(("parallel",)),
        name="ssm_coefs",
    )(log_dt.reshape(nb, GBLK, 1), blk(a_re), blk(a_im), blk(bt_re), blk(bt_im),
      blk(c_re), blk(c_im))


def _chunk_rows(u_ref, lc):
    nk = u_ref.shape[0] // lc
    return jnp.concatenate([u_ref[pl.ds(s, nk, stride=lc), :] for s in range(lc)], axis=1)


def _ssm_local_kernel(u_ref, s_ref, o_ref, *, lc):
    o_ref[0] = jnp.dot(_chunk_rows(u_ref, lc).astype(BF16), s_ref[0],
                       preferred_element_type=F32)


def _ssm_local(u, smat, lc):
    t = u.shape[0]
    nb, w, n = smat.shape
    return pl.pallas_call(
        functools.partial(_ssm_local_kernel, lc=lc),
        grid=(nb,),
        in_specs=[pl.BlockSpec((t, LANES), lambda i: (0, i)),
                  pl.BlockSpec((1, w, n), lambda i: (i, 0, 0))],
        out_specs=pl.BlockSpec((1, t // lc, n), lambda i: (i, 0, 0)),
        out_shape=jax.ShapeDtypeStruct((nb, t // lc, n), F32),
        compiler_params=_cparams(("parallel",)),
        name="ssm_local",
    )(u, smat)


def _ssm_rec_kernel(s_ref, are_ref, aim_ref, h_ref, fin_ref, st_sc):
    kc = s_ref.shape[0]
    w = are_ref.shape[1]
    a_re = are_ref[...]
    a_im = aim_ref[...]

    @pl.when(pl.program_id(0) == 0)
    def _init():
        st_sc[...] = jnp.zeros(st_sc.shape, F32)

    def body(k, carry):
        hr, hi = carry
        h_ref[k] = jnp.concatenate([hr, hi], axis=1)
        sk = s_ref[k]
        return (a_re * hr - a_im * hi + sk[:, 0:w], a_re * hi + a_im * hr + sk[:, w:2 * w])

    hr, hi = lax.fori_loop(0, kc, body, (st_sc[0], st_sc[1]))
    st_sc[0] = hr
    st_sc[1] = hi
    fin_ref[...] = jnp.concatenate([hr, hi], axis=1)


def _ssm_rec(s_kg, apow_re, apow_im):
    nk, gp, w2 = s_kg.shape
    kc = min(nk, 256)
    return pl.pallas_call(
        _ssm_rec_kernel,
        grid=(nk // kc,),
        in_specs=[pl.BlockSpec((kc, gp, w2), lambda i: (i, 0, 0)),
                  pl.BlockSpec((gp, w2 // 2), lambda i: (0, 0)),
                  pl.BlockSpec((gp, w2 // 2), lambda i: (0, 0))],
        out_specs=[pl.BlockSpec((kc, gp, w2), lambda i: (i, 0, 0)),
                   pl.BlockSpec((gp, w2), lambda i: (0, 0))],
        out_shape=[jax.ShapeDtypeStruct((nk, gp, w2), F32),
                   jax.ShapeDtypeStruct((gp, w2), F32)],
        scratch_shapes=[pltpu.VMEM((2, gp, w2 // 2), F32)],
        compiler_params=_cparams(("arbitrary",)),
        name="ssm_rec",
    )(s_kg, apow_re, apow_im)


def _ssm_out_kernel(u_ref, t_ref, h_ref, m_ref, d_ref, o_ref, *, lc):
    nk = u_ref.shape[0] // lc
    y = jnp.dot(_chunk_rows(u_ref, lc).astype(BF16), t_ref[0], preferred_element_type=F32)
    y = y + jnp.dot(h_ref[0].astype(BF16), m_ref[0], preferred_element_type=F32)
    for r in range(lc):
        rows = pl.ds(r, nk, stride=lc)
        o_ref[rows, :] = y[:, r * LANES:(r + 1) * LANES] + d_ref[...] * u_ref[rows, :]


def _ssm_out(u, tmat, hst, mmat, d_row, lc):
    t = u.shape[0]
    nb, w, _ = tmat.shape
    hw = hst.shape[2]
    return pl.pallas_call(
        functools.partial(_ssm_out_kernel, lc=lc),
        grid=(nb,),
        in_specs=[pl.BlockSpec((t, LANES), lambda i: (0, i)),
                  pl.BlockSpec((1, w, w), lambda i: (i, 0, 0)),
                  pl.BlockSpec((1, t // lc, hw), lambda i: (i, 0, 0)),
                  pl.BlockSpec((1, hw, w), lambda i: (i, 0, 0)),
                  pl.BlockSpec((1, LANES), lambda i: (0, i))],
        out_specs=pl.BlockSpec((t, LANES), lambda i: (0, i)),
        out_shape=jax.ShapeDtypeStruct(u.shape, F32),
        compiler_params=_cparams(("parallel",)),
        name="ssm_out",
    )(u, tmat, hst, mmat, d_row)


def _ssm_prompt(u, coefs, ssm_d):
    tmat, smat, mmat, _, _, apow_re, apow_im, _, _ = coefs
    g, p, lc = N_SSM_GROUPS, SSM_STATE, SCAN_CHUNK
    nb = g // GBLK
    hw = GBLK * p
    s_loc = _ssm_local(u, smat, lc)
    hst, fin = _ssm_rec(s_loc.transpose(1, 0, 2), apow_re.reshape(nb, hw),
                        apow_im.reshape(nb, hw))
    y = _ssm_out(u, tmat, hst.transpose(1, 0, 2), mmat, ssm_d.reshape(1, SSM_WIDTH), lc)
    return y, fin[:, :hw].reshape(g, p), fin[:, hw:].reshape(g, p)


def _ssm_sample_kernel(u_ref, hr_ref, hi_ref, are_ref, aim_ref, bbr_ref, bbi_ref,
                       cr_ref, ci_ref, d_ref, y_ref, fr_ref, fi_ref, *, nb, steps):
    nt = (((1,), (0,)), ((), ()))
    ntt = (((1,), (1,)), ((), ()))
    for gi in range(u_ref.shape[0]):
        u = u_ref[gi]
        bu_re = lax.dot_general(u, bbr_ref[gi], nt, precision=_HI, preferred_element_type=F32)
        bu_im = lax.dot_general(u, bbi_ref[gi], nt, precision=_HI, preferred_element_type=F32)
        a_re, a_im = are_ref[gi], aim_ref[gi]
        hr, hi = hr_ref[gi], hi_ref[gi]
        hs_re, hs_im = [], []
        for l in range(steps):
            sl = slice(l * nb, (l + 1) * nb)
            hr, hi = (a_re * hr - a_im * hi + bu_re[sl], a_re * hi + a_im * hr + bu_im[sl])
            hs_re.append(hr)
            hs_im.append(hi)
        h_re = jnp.concatenate(hs_re, axis=0).astype(BF16)
        h_im = jnp.concatenate(hs_im, axis=0).astype(BF16)
        y = (lax.dot_general(h_re, cr_ref[gi].astype(BF16), ntt, preferred_element_type=F32)
             - lax.dot_general(h_im, ci_ref[gi].astype(BF16), ntt, preferred_element_type=F32))
        y_ref[gi] = y + d_ref[gi] * u
        fr_ref[gi] = hr
        fi_ref[gi] = hi


def _ssm_sample(u_g, h0_re, h0_im, coefs, c_re, c_im, ssm_d, nb, steps):
    g, p, c = N_SSM_GROUPS, SSM_STATE, SSM_GROUP
    abar_re, abar_im = coefs[3].reshape(g, 1, p), coefs[4].reshape(g, 1, p)
    bb_re, bb_im = coefs[7].reshape(g, c, p), coefs[8].reshape(g, c, p)
    rows = steps * nb
    sds = jax.ShapeDtypeStruct
    spec = lambda a, b: pl.BlockSpec((GBLK, a, b), lambda i: (i, 0, 0))
    return pl.pallas_call(
        functools.partial(_ssm_sample_kernel, nb=nb, steps=steps),
        grid=(g // GBLK,),
        in_specs=[spec(rows, c), spec(nb, p), spec(nb, p), spec(1, p), spec(1, p),
                  spec(c, p), spec(c, p), spec(c, p), spec(c, p), spec(1, c)],
        out_specs=[spec(rows, c), spec(nb, p), spec(nb, p)],
        out_shape=[sds((g, rows, c), F32), sds((g, nb, p), F32), sds((g, nb, p), F32)],
        compiler_params=_cparams(("parallel",)),
        name="ssm_sample",
    )(u_g, h0_re, h0_im, abar_re, abar_im, bb_re, bb_im, c_re, c_im,
      ssm_d.reshape(g, 1, c))


def _layer_norm(x, g, b):
    mu = jnp.mean(x, axis=-1, keepdims=True)
    var = jnp.mean(jnp.square(x - mu), axis=-1, keepdims=True)
    return (x - mu) * lax.rsqrt(var + LN_EPS) * g + b


def _merge_kernel(o_ref, y_ref, x_ref, gw_ref, gb_ref, wo_ref, g_ref, b_ref,
                  x1_ref, x1t_ref):
    z = jax.nn.gelu(y_ref[...])
    gate = jnp.dot(z.astype(BF16), gw_ref[...], preferred_element_type=F32) + gb_ref[...]
    z = z * jax.nn.sigmoid(gate)
    mix = jnp.dot(o_ref[...].astype(BF16), wo_ref[0:ATTN_WIDTH, :], preferred_element_type=F32)
    mix = mix + jnp.dot(z.astype(BF16), wo_ref[ATTN_WIDTH:, :], preferred_element_type=F32)
    x1 = _layer_norm(DEEPNORM_ALPHA * x_ref[...] + mix, g_ref[...], b_ref[...])
    x1_ref[...] = x1
    x1t_ref[...] = x1.T.astype(BF16)


def _merge(attn_o, y_ssm, x, glu_w, glu_b, w_out, ln_g, ln_b, tm):
    t = x.shape[0]
    full = lambda r, c: pl.BlockSpec((r, c), lambda i: (0, 0), pipeline_mode=pl.Buffered(1))
    return pl.pallas_call(
        _merge_kernel,
        grid=(t // tm,),
        in_specs=[pl.BlockSpec((tm, ATTN_WIDTH), lambda i: (i, 0)),
                  pl.BlockSpec((tm, SSM_WIDTH), lambda i: (i, 0)),
                  pl.BlockSpec((tm, D_MODEL), lambda i: (i, 0)),
                  full(SSM_WIDTH, SSM_WIDTH), full(1, SSM_WIDTH),
                  full(D_MODEL, D_MODEL), full(1, D_MODEL), full(1, D_MODEL)],
        out_specs=[pl.BlockSpec((tm, D_MODEL), lambda i: (i, 0)),
                   pl.BlockSpec((D_MODEL, tm), lambda i: (0, i))],
        out_shape=[jax.ShapeDtypeStruct((t, D_MODEL), F32),
                   jax.ShapeDtypeStruct((D_MODEL, t), BF16)],
        compiler_params=_cparams(("parallel",)),
        name="merge_out_ln1",
    )(attn_o, y_ssm, x, glu_w, glu_b, w_out, ln_g, ln_b)


def _top_desc(vals, n):
    rows = []
    for r in range(n):
        mx = jnp.max(vals, axis=0, keepdims=True)
        rows.append(mx)
        if r + 1 < n:
            vals = jnp.where(vals == mx, -jnp.inf, vals)
    return rows


def _route_kernel(xt_ref, wq_ref, keys_ref, e1_ref, e2_ref, te_ref):
    k = PEER_TOPK
    qt = jnp.dot(wq_ref[...], xt_ref[...], preferred_element_type=F32).astype(BF16)
    for h in range(PEER_HEADS):
        tops = []
        for m in range(2):
            hm = 2 * h + m
            sc = jnp.dot(keys_ref[hm], qt[hm * PEER_NKEYS:(hm + 1) * PEER_NKEYS, :],
                         preferred_element_type=F32)
            tops.append((sc, _top_desc(sc, k + 1)))
        (sc1, a), (sc2, b) = tops
        ninf = jnp.full_like(a[0], -jnp.inf)
        pad = [ninf] * (-(k + 1) % 8)
        amat = jnp.concatenate(a + pad, axis=0)
        bmat = jnp.concatenate(b + pad, axis=0)
        row8 = lax.broadcasted_iota(jnp.int32, (8, amat.shape[1]), 0)
        groups = [a[0] + bmat]
        for r1 in range(2, 9):
            groups.append(jnp.where(row8 < (k + 1) // r1, a[r1 - 1] + bmat[0:8], -jnp.inf))
        groups.append(amat[8:] + b[0])
        cand = jnp.concatenate(groups, axis=0)
        best = _top_desc(cand, k + 1)
        zsum = jnp.ones_like(best[0])
        for r in range(1, k):
            zsum = zsum + jnp.exp(best[r] - best[0])
        theta = 0.5 * (best[k - 1] + best[k])
        e1_ref[h] = jnp.exp(sc1 - a[0]) / zsum
        e2_ref[h] = jnp.exp(sc2 - b[0])
        te_ref[h] = jnp.exp((theta - b[0]) - sc1)


def _route(x1t, wq_t, keys, tm):
    t = x1t.shape[1]
    big = pl.BlockSpec((PEER_HEADS, PEER_NKEYS, tm), lambda i: (0, 0, i))
    sds = jax.ShapeDtypeStruct((PEER_HEADS, PEER_NKEYS, t), F32)
    return pl.pallas_call(
        _route_kernel,
        grid=(t // tm,),
        in_specs=[pl.BlockSpec((D_MODEL, tm), lambda i: (0, i)),
                  pl.BlockSpec(wq_t.shape, lambda i: (0, 0)),
                  pl.BlockSpec(keys.shape, lambda i: (0, 0, 0))],
        out_specs=[big, big, big],
        out_shape=[sds, sds, sds],
        compiler_params=_cparams(("parallel",)),
        name="peer_route",
    )(x1t, wq_t, keys)


IPB = 4
UNIT_I = 2
ATTN_ROW_CHUNK = 256


def _peer_kernel(xt_ref, u_ref, vt_ref, e1_ref, e2_ref, te_ref, o_ref, *, tsub):
    eb = pl.program_id(1)

    @pl.when(eb == 0)
    def _zero():
        o_ref[...] = jnp.zeros(o_ref.shape, F32)

    n_sub = xt_ref.shape[1] // tsub
    upb = IPB // UNIT_I
    units = [(j, un) for j in range(n_sub) for un in range(upb)]
    urows = UNIT_I * PEER_NKEYS
    vrows = D_MODEL // upb

    u_units = [u_ref[un * urows:(un + 1) * urows, :].astype(BF16) for un in range(upb)]

    def up(j, un):
        return jnp.dot(u_units[un], xt_ref[:, j * tsub:(j + 1) * tsub],
                       preferred_element_type=F32)

    def down_piece(j, a, piece):
        rows = slice(piece * vrows, (piece + 1) * vrows)
        o_ref[rows, j * tsub:(j + 1) * tsub] += jnp.dot(vt_ref[rows, :], a,
                                                        preferred_element_type=F32)

    def gate(j, ii, ht):
        i = eb * IPB + ii
        cols = slice(j * tsub, (j + 1) * tsub)
        e1rows = [e1_ref[h, pl.ds(i, 1), cols] for h in range(PEER_HEADS)]
        terows = [te_ref[h, pl.ds(i, 1), cols] for h in range(PEER_HEADS)]
        a_cols = []
        for c0 in range(0, tsub, LANES):
            hl = slice(c0, c0 + LANES)
            lc = slice(j * tsub + c0, j * tsub + c0 + LANES)
            w = None
            for h in range(PEER_HEADS):
                e2 = e2_ref[h, :, lc]
                wh = jnp.where(e2 >= terows[h][:, hl], e2, 0.0) * e1rows[h][:, hl]
                w = wh if w is None else w + wh
            a_cols.append((w * jax.nn.gelu(ht[:, hl])).astype(BF16))
        return a_cols[0] if len(a_cols) == 1 else jnp.concatenate(a_cols, axis=1)

    ht_next = up(*units[0])
    a_prev, a_rows = None, []
    for n, (j, un) in enumerate(units):
        ht = ht_next
        if n + 1 < len(units):
            ht_next = up(*units[n + 1])
        if a_prev is not None:
            down_piece(j - 1, a_prev, un)
        for k in range(UNIT_I):
            a_rows.append(gate(j, un * UNIT_I + k, ht[k * PEER_NKEYS:(k + 1) * PEER_NKEYS, :]))
        if un == upb - 1:
            a_prev, a_rows = jnp.concatenate(a_rows, axis=0), []
    for piece in range(upb):
        down_piece(n_sub - 1, a_prev, piece)


def _peer(x1t, u_tab, vt_tab, e1, e2, te_thr, tm, tsub):
    t = x1t.shape[1]
    te = IPB * PEER_NKEYS
    n_eb = u_tab.shape[0] // te
    once = pl.Buffered(1)
    big = pl.BlockSpec((PEER_HEADS, PEER_NKEYS, tm), lambda i, e: (0, 0, i), pipeline_mode=once)
    return pl.pallas_call(
        functools.partial(_peer_kernel, tsub=tsub),
        grid=(t // tm, n_eb),
        in_specs=[pl.BlockSpec((D_MODEL, tm), lambda i, e: (0, i), pipeline_mode=once),
                  pl.BlockSpec((te, D_MODEL), lambda i, e: (e, 0)),
                  pl.BlockSpec((D_MODEL, te), lambda i, e: (0, e)),
                  big, big, big],
        out_specs=pl.BlockSpec((D_MODEL, tm), lambda i, e: (0, i)),
        out_shape=jax.ShapeDtypeStruct((D_MODEL, t), F32),
        compiler_params=_cparams(("parallel", "arbitrary")),
        name="peer_experts",
    )(x1t, u_tab, vt_tab, e1, e2, te_thr)


def _ln2_kernel(x1_ref, ft_ref, g_ref, b_ref, o_ref):
    o_ref[...] = _layer_norm(DEEPNORM_ALPHA * x1_ref[...] + ft_ref[...].T,
                             g_ref[...], b_ref[...])


def _ln2(x1, ffn_t, ln_g, ln_b, tm):
    t = x1.shape[0]
    return pl.pallas_call(
        _ln2_kernel,
        grid=(t // tm,),
        in_specs=[pl.BlockSpec((tm, D_MODEL), lambda i: (i, 0)),
                  pl.BlockSpec((D_MODEL, tm), lambda i: (0, i)),
                  pl.BlockSpec((1, D_MODEL), lambda i: (0, 0)),
                  pl.BlockSpec((1, D_MODEL), lambda i: (0, 0))],
        out_specs=pl.BlockSpec((tm, D_MODEL), lambda i: (i, 0)),
        out_shape=jax.ShapeDtypeStruct((t, D_MODEL), F32),
        compiler_params=_cparams(("parallel",)),
        name="ln2",
    )(x1, ffn_t, ln_g, ln_b)


def _sample_buckets(past_len, dec_seq):
    qpos = past_len + np.tile(np.arange(dec_seq), 2)[:, None]
    w = PPS * PAGE_SIZE
    kpos = (past_len - w) + np.arange(w)[None, :]
    last = _bucket_np(qpos - kpos)
    jn = np.arange(NEWK)[None, :]
    knew = past_len + jn
    new = np.where((jn < dec_seq) & (knew <= qpos), _bucket_np(qpos - knew), -1)
    return last[None].astype(np.int32), new[None].astype(np.int32)


def _group_tail(x1, x1t, w_query_t, keys, u_tab, vt_tab, ln_g, ln_b, tm_route, tm_peer, tm_ln):
    e1, e2, te_thr = _route(x1t, w_query_t, keys, tm_route)
    ffn_t = _peer(x1t, u_tab, vt_tab, e1, e2, te_thr, tm_peer, min(tm_peer, 256))
    return _ln2(x1, ffn_t, ln_g, ln_b, tm_ln)


def kernel(x_prompt, x_sample, cache_k, cache_v, state_ssm_re, state_ssm_im, page_table, w_in, lambda_q1, lambda_k1, lambda_q2, lambda_k2, attn_subln_g, rel_bias, ssm_a_re, ssm_a_im, ssm_b_re, ssm_b_im, ssm_c_re, ssm_c_im, ssm_d, ssm_log_dt, ssm_glu_w, ssm_glu_b, w_out, ln1_g, ln1_b, peer_w_query, peer_sub_keys, peer_u, peer_v, ln2_g, ln2_b):
    l = 0
    seq = x_prompt.shape[1]
    dec_b, dec_s = x_sample.shape[0], x_sample.shape[1]
    n_pages = page_table.shape[1]
    past_len = n_pages * PAGE_SIZE
    g, p, c = N_SSM_GROUPS, SSM_STATE, SSM_GROUP
    lam_init = 0.8 - 0.6 * math.exp(-0.3 * l)
    lam = (jnp.exp(jnp.sum(lambda_q1[l] * lambda_k1[l]))
           - jnp.exp(jnp.sum(lambda_q2[l] * lambda_k2[l])) + lam_init).reshape(1).astype(F32)

    w_in_b = w_in[l].astype(BF16)
    glu_w_b = ssm_glu_w[l].astype(BF16)
    w_out_b = w_out[l].astype(BF16)
    wq_t = peer_w_query[l].T.astype(BF16)
    keys = peer_sub_keys[l].reshape(2 * PEER_HEADS, PEER_NKEYS, PEER_NKEYS).astype(BF16)
    u_tab = peer_u[l]
    vt_tab = peer_v[l].T.astype(BF16)
    subln_g = attn_subln_g[l].reshape(1, HEAD_DIM)
    glu_b = ssm_glu_b[l].reshape(1, SSM_WIDTH)
    g1, b1 = ln1_g[l].reshape(1, D_MODEL), ln1_b[l].reshape(1, D_MODEL)
    g2, b2 = ln2_g[l].reshape(1, D_MODEL), ln2_b[l].reshape(1, D_MODEL)
    coefs = _ssm_coefs(ssm_log_dt[l], ssm_a_re[l], ssm_a_im[l],
                       ssm_b_re[l].transpose(0, 2, 1), ssm_b_im[l].transpose(0, 2, 1),
                       ssm_c_re[l], ssm_c_im[l])

    xp = x_prompt.reshape(seq, D_MODEL)
    tb = 1024
    q_p, k_p, v_p, u_p = _proj(xp, w_in_b, 512, ATTN_WIDTH)
    bias_p = _band_bias(rel_bias, tb, LOG2E)
    attn_p = _attn_prompt(q_p, k_p, v_p, bias_p, lam, subln_g, tb, lam_init)
    y_p, hr_p, hi_p = _ssm_prompt(u_p, coefs, ssm_d[l])
    x1_p, x1t_p = _merge(attn_p, y_p, xp, glu_w_b, glu_b, w_out_b, g1, b1, 512)
    out_p = _group_tail(x1_p, x1t_p, wq_t, keys, u_tab, vt_tab, g2, b2, 256, 1024, 256)

    ts = dec_b * dec_s
    xs = x_sample.reshape(ts, D_MODEL)
    qkvu_s = _proj(xs, w_in_b, ts, ATTN_WIDTH)
    q_s = qkvu_s[0].reshape(dec_b, dec_s, N_HEADS, HEAD_DIM) * (HALF_DIM ** -0.5)
    lane = np.arange(HEAD_DIM)
    halves = jnp.asarray(np.stack([lane < HALF_DIM, lane >= HALF_DIM]).astype(np.float32))
    q2 = (q_s[:, None] * halves[None, :, None, None, :])
    q2 = q2.transpose(0, 3, 1, 2, 4).reshape(dec_b, N_HEADS, QROWS, HEAD_DIM)
    pad_new = lambda a: jnp.pad(
        a.reshape(dec_b, dec_s, N_HEADS, HEAD_DIM).transpose(0, 2, 1, 3),
        ((0, 0), (0, 0), (0, NEWK - dec_s), (0, 0)))
    bl, bn = _sample_buckets(past_len, dec_s)
    bias_last = _bias_tiles(rel_bias, bl)[:, 0]
    bias_new = _bias_tiles(rel_bias, bn)[:, 0]
    attn_s = _attn_sample(q2, pad_new(qkvu_s[1]), pad_new(qkvu_s[2]), cache_k[l], cache_v[l],
                          page_table, bias_last, bias_new, lam, subln_g, lam_init)
    u_s = qkvu_s[3].reshape(dec_b, dec_s, g, c).transpose(2, 1, 0, 3).reshape(g, ts, c)
    y_g, hr_s, hi_s = _ssm_sample(u_s, state_ssm_re[l].transpose(1, 0, 2),
                                  state_ssm_im[l].transpose(1, 0, 2), coefs,
                                  ssm_c_re[l], ssm_c_im[l], ssm_d[l], dec_b, dec_s)
    y_s = y_g.reshape(g, dec_s, dec_b, c).transpose(2, 1, 0, 3).reshape(ts, SSM_WIDTH)
    x1_s, x1t_s = _merge(attn_s.reshape(ts, ATTN_WIDTH), y_s, xs, glu_w_b, glu_b, w_out_b,
                         g1, b1, ts)
    out_s = _group_tail(x1_s, x1t_s, wq_t, keys, u_tab, vt_tab, g2, b2, ts, ts, ts)

    kv = lambda a, b_, s_: a.reshape(1, b_, s_, N_HEADS, HEAD_DIM)
    return (out_p.reshape(1, seq, D_MODEL), out_s.reshape(dec_b, dec_s, D_MODEL),
            kv(k_p, 1, seq), kv(v_p, 1, seq),
            hr_p.reshape(1, 1, g, p), hi_p.reshape(1, 1, g, p),
            kv(qkvu_s[1], dec_b, dec_s), kv(qkvu_s[2], dec_b, dec_s),
            hr_s.transpose(1, 0, 2)[None], hi_s.transpose(1, 0, 2)[None])
```
